```python
import jax, jax.numpy as jnp
from jax import lax
import numpy as np

D_MODEL = 1024
BATCH = 32
SEQ = 256
DEPTH = 4
DEC_BATCH = 2
DEC_SEQ = 2048
PAST_LEN = 256

GRID_W = 64
N_HEADS = 8
HEAD_DIM = 64
ATTN_WIDTH = N_HEADS * HEAD_DIM
CONV_WIDTH = D_MODEL // 2
CONV_K = 3
D_FF = 4 * D_MODEL
WIN_R = 8
WIN_C = 16
Q_BLOCK = 128
IN_COLS = 3 * ATTN_WIDTH + 3 * CONV_WIDTH + 2 * D_MODEL
SPLIT_POINTS = (ATTN_WIDTH, 2 * ATTN_WIDTH, 3 * ATTN_WIDTH,
                3 * ATTN_WIDTH + CONV_WIDTH, 3 * ATTN_WIDTH + 2 * CONV_WIDTH,
                3 * ATTN_WIDTH + 3 * CONV_WIDTH, 3 * ATTN_WIDTH + 3 * CONV_WIDTH + D_MODEL)
SCALE = HEAD_DIM ** -0.5
ALPHA = (2.0 * DEPTH) ** 0.25
BETA = (8.0 * DEPTH) ** -0.25
LN_EPS = 1e-5

kernel_name = "hybrid_dit_natten_shortconv_step"


def _layer_norm(x, g, b):
    xf = x.astype(jnp.float32)
    mu = jnp.mean(xf, axis=-1, keepdims=True)
    var = jnp.mean(jnp.square(xf - mu), axis=-1, keepdims=True)
    return ((xf - mu) * lax.rsqrt(var + LN_EPS)).astype(x.dtype) * g + b


def _heads(t):
    b, l, _ = t.shape
    return t.reshape(b, l, N_HEADS, HEAD_DIM).transpose(0, 2, 1, 3)


def _merge_heads(t):
    b, h, l, d = t.shape
    return t.transpose(0, 2, 1, 3).reshape(b, l, h * d)


def _short_conv(z, w, bias):
    zp = jnp.pad(z, ((0, 0), (1, 1), (0, 0)))
    return zp[:, :-2] * w[0] + zp[:, 1:-1] * w[1] + zp[:, 2:] * w[2] + bias


def _context_attention(q, k, v):
    b, h, l, d = q.shape
    nb = l // Q_BLOCK
    qb = q.reshape(b, h, nb, Q_BLOCK, d).transpose(2, 0, 1, 3, 4)

    def blk(qi):
        s = jnp.einsum('bhqd,bhkd->bhqk', qi, k).astype(jnp.float32) * SCALE
        p = jax.nn.softmax(s, axis=-1).astype(v.dtype)
        return jnp.einsum('bhqk,bhkd->bhqd', p, v)

    o = lax.map(blk, qb)
    return o.transpose(1, 2, 0, 3, 4).reshape(b, h, l, d)


def _latent_neighbourhood_attention(q, k, v, k_ctx, v_ctx, rpb):
    b, h, t, d = q.shape
    rows = t // GRID_W
    kh = min(WIN_R, rows)
    qg = q.reshape(b, h, rows, GRID_W, d)
    kg = k.reshape(b, h, rows, GRID_W, d)
    vg = v.reshape(b, h, rows, GRID_W, d)
    cols = jnp.arange(GRID_W)
    col_start = jnp.clip(cols - WIN_C // 2, 0, GRID_W - WIN_C)
    col_mask = (cols[None, :] >= col_start[:, None]) & (cols[None, :] < col_start[:, None] + WIN_C)
    col_idx = jnp.clip(cols[None, :] - cols[:, None] + WIN_C - 1, 0, 2 * WIN_C - 2)
    rpb_cols = rpb[:, :, col_idx]
    n_band = kh * GRID_W

    def row_fn(args):
        q_r, r = args
        sr = jnp.clip(r - kh // 2, 0, rows - kh)
        k_b = lax.dynamic_slice_in_dim(kg, sr, kh, axis=2)
        v_b = lax.dynamic_slice_in_dim(vg, sr, kh, axis=2)
        row_idx = sr + jnp.arange(kh) - r + WIN_R - 1
        bias = rpb_cols[:, row_idx].transpose(0, 2, 1, 3)
        s_band = jnp.einsum('bhqd,bhikd->bhqik', q_r, k_b).astype(jnp.float32) * SCALE
        s_band = s_band + bias[None].astype(jnp.float32)
        s_band = jnp.where(col_mask[None, None, :, None, :], s_band, -jnp.inf)
        s_ctx = jnp.einsum('bhqd,bhld->bhql', q_r, k_ctx).astype(jnp.float32) * SCALE
        s = jnp.concatenate([s_band.reshape(b, h, GRID_W, n_band), s_ctx], axis=-1)
        p = jax.nn.softmax(s, axis=-1).astype(v.dtype)
        p_band = p[..., :n_band].reshape(b, h, GRID_W, kh, GRID_W)
        p_ctx = p[..., n_band:]
        return (jnp.einsum('bhqik,bhikd->bhqd', p_band, v_b)
                + jnp.einsum('bhql,bhld->bhqd', p_ctx, v_ctx))

    o = lax.map(row_fn, (qg.transpose(2, 0, 1, 3, 4), jnp.arange(rows)))
    return o.transpose(1, 2, 0, 3, 4).reshape(b, h, t, d)


def _layer(x, cond, attend, w_mod, b_mod, w_in, conv_w, conv_b, w_attn_proj, w_conv_proj, w_o,
           ln1_g, ln1_b, w1, b1, w2, b2, ln2_g, ln2_b):
    mod = (jax.nn.silu(cond) @ w_mod + b_mod)[:, None, :]
    sh1, sc1, g1, sh2, sc2, g2 = jnp.split(mod, 6, axis=-1)
    h = x * (1 + sc1) + sh1
    q, k, v, u, bg, cg, ga, gc = jnp.split(h @ w_in, SPLIT_POINTS, axis=-1)
    k = _heads(k)
    v = _heads(v)
    attn = _merge_heads(attend(_heads(q), k, v)) @ w_attn_proj
    conv = (bg * _short_conv(cg * u, conv_w, conv_b)) @ w_conv_proj
    mix = (jax.nn.sigmoid(ga) * attn + jax.nn.sigmoid(gc) * conv) @ w_o
    x = _layer_norm(ALPHA * x + g1 * mix, ln1_g, ln1_b)
    h = x * (1 + sc2) + sh2
    f = jnp.square(jax.nn.relu(h @ w1 + b1)) @ w2 + b2
    x = _layer_norm(ALPHA * x + g2 * f, ln2_g, ln2_b)
    return x, k, v


def setup_inputs(seed: int = 0) -> dict:
    key = jax.random.key(seed)
    ks = jax.random.split(key, 32)
    n = jax.random.normal
    f32 = jnp.float32
    d = D_MODEL
    return {
        'x_prompt': n(ks[0], (BATCH, SEQ, d), f32),
        'x_sample': n(ks[1], (DEC_BATCH, DEC_SEQ, d), f32),
        'cache_k': n(ks[2], (DEC_BATCH, DEPTH, N_HEADS, PAST_LEN, HEAD_DIM), f32),
        'cache_v': n(ks[3], (DEC_BATCH, DEPTH, N_HEADS, PAST_LEN, HEAD_DIM), f32),
        'c': n(ks[4], (DEC_BATCH, d), f32),
        'c_ctx': n(ks[5], (d,), f32),
        'w_mod': n(ks[6], (DEPTH, d, 6 * d), f32) * (0.5 * d ** -0.5),
        'b_mod': n(ks[7], (DEPTH, 6 * d), f32) * 0.01,
        'w_in': n(ks[8], (DEPTH, d, IN_COLS), f32) * d ** -0.5,
        'rpb': n(ks[9], (DEPTH, N_HEADS, 2 * WIN_R - 1, 2 * WIN_C - 1), f32) * 0.5,
        'conv_w': n(ks[10], (DEPTH, CONV_K, CONV_WIDTH), f32) * CONV_K ** -0.5,
        'conv_b': n(ks[11], (DEPTH, CONV_WIDTH), f32) * 0.01,
        'w_attn_proj': n(ks[12], (DEPTH, ATTN_WIDTH, d), f32) * ATTN_WIDTH ** -0.5,
        'w_conv_proj': n(ks[13], (DEPTH, CONV_WIDTH, d), f32) * CONV_WIDTH ** -0.5,
        'w_o': n(ks[14], (DEPTH, d, d), f32) * (BETA * d ** -0.5),
        'ln1_g': 1.0 + 0.01 * n(ks[15], (DEPTH, d), f32),
        'ln1_b': 0.01 * n(ks[16], (DEPTH, d), f32),
        'w1': n(ks[17], (DEPTH, d, D_FF), f32) * d ** -0.5,
        'b1': 0.01 * n(ks[18], (DEPTH, D_FF), f32),
        'w2': n(ks[19], (DEPTH, D_FF, d), f32) * (BETA * D_FF ** -0.5),
        'b2': 0.01 * n(ks[20], (DEPTH, d), f32),
        'ln2_g': 1.0 + 0.01 * n(ks[21], (DEPTH, d), f32),
        'ln2_b': 0.01 * n(ks[22], (DEPTH, d), f32),
    }


def reference(x_prompt, x_sample, cache_k, cache_v, c, c_ctx, w_mod, b_mod, w_in, rpb, conv_w, conv_b,
              w_attn_proj, w_conv_proj, w_o, ln1_g, ln1_b, w1, b1, w2, b2, ln2_g, ln2_b):
    xp = x_prompt
    cond_ctx = c_ctx[None, :]
    ks_list = []
    vs_list = []
    for l in range(DEPTH):
        xp, k_l, v_l = _layer(xp, cond_ctx, _context_attention, w_mod[l], b_mod[l], w_in[l], conv_w[l],
                              conv_b[l], w_attn_proj[l], w_conv_proj[l], w_o[l], ln1_g[l], ln1_b[l],
                              w1[l], b1[l], w2[l], b2[l], ln2_g[l], ln2_b[l])
        ks_list.append(k_l)
        vs_list.append(v_l)
    new_k = jnp.stack(ks_list, axis=1)
    new_v = jnp.stack(vs_list, axis=1)

    xs = x_sample
    for l in range(DEPTH):
        k_ctx = cache_k[:, l]
        v_ctx = cache_v[:, l]
        rpb_l = rpb[l]
        attend = lambda q, k, v, k_ctx=k_ctx, v_ctx=v_ctx, rpb_l=rpb_l: _latent_neighbourhood_attention(
            q, k, v, k_ctx, v_ctx, rpb_l)
        xs, _, _ = _layer(xs, c, attend, w_mod[l], b_mod[l], w_in[l], conv_w[l], conv_b[l],
                          w_attn_proj[l], w_conv_proj[l], w_o[l], ln1_g[l], ln1_b[l],
                          w1[l], b1[l], w2[l], b2[l], ln2_g[l], ln2_b[l])
    return (xp, xs, new_k, new_v)
```

```python
import functools

import jax
import jax.numpy as jnp
from jax import lax
from jax.experimental import pallas as pl
from jax.experimental.pallas import tpu as pltpu

D_MODEL = 1024
DEPTH = 4
N_HEADS = 8
HEAD_DIM = 64
ATTN_WIDTH = N_HEADS * HEAD_DIM
CONV_WIDTH = D_MODEL // 2
D_FF = 4 * D_MODEL
GRID_W = 64
WIN_R = 8
WIN_C = 16
RPB_ROWS = 2 * WIN_R - 1
RPB_COLS = 2 * WIN_C - 1
IN_COLS = 3 * ATTN_WIDTH + 3 * CONV_WIDTH + 2 * D_MODEL
SCALE = HEAD_DIM ** -0.5
ALPHA = (2.0 * DEPTH) ** 0.25
LN_EPS = 1e-5

COND_ROWS = 8
TOKEN_TILE = 256
HALO_ROWS = 16
ROWS_PER_CHUNK = 4
VMEM_LIMIT_BYTES = 56 * 1024 * 1024

BF16 = jnp.bfloat16
F32 = jnp.float32
NT_DIMS = (((1,), (1,)), ((), ()))


def _dot(a, b):
    return jnp.dot(a, b, preferred_element_type=F32)


def _dot_nt(a, b):
    return lax.dot_general(a, b, NT_DIMS, preferred_element_type=F32)


def _resident(shape):
    zeros = (0,) * len(shape)
    return pl.BlockSpec(shape, lambda *_: zeros, pipeline_mode=pl.Buffered(1))


def _cond_row(tile, mod_row0, tiles_per_seq):
    if mod_row0 == 0:
        return 0
    return mod_row0 + tile // tiles_per_seq


def _params(n_grid_dims):
    return pltpu.CompilerParams(
        dimension_semantics=("arbitrary",) * n_grid_dims,
        vmem_limit_bytes=VMEM_LIMIT_BYTES,
    )


MOD_COL_TILE = 1536


def _mod_kernel(cond_ref, w_ref, b_ref, o_ref):
    c = cond_ref[...]
    s = (c * jax.nn.sigmoid(c)).astype(BF16)
    o_ref[0] = _dot(s, w_ref[0].astype(BF16)) + b_ref[0]


def _modulation(cond, w_mod, b_mod):
    n_cols = 6 * D_MODEL
    return pl.pallas_call(
        _mod_kernel,
        grid=(DEPTH, n_cols // MOD_COL_TILE),
        in_specs=[
            pl.BlockSpec((COND_ROWS, D_MODEL), lambda l, j: (0, 0)),
            pl.BlockSpec((1, D_MODEL, MOD_COL_TILE), lambda l, j: (l, 0, j)),
            pl.BlockSpec((1, 1, MOD_COL_TILE), lambda l, j: (l, 0, j)),
        ],
        out_specs=pl.BlockSpec((1, COND_ROWS, MOD_COL_TILE), lambda l, j: (l, 0, j)),
        out_shape=jax.ShapeDtypeStruct((DEPTH, COND_ROWS, n_cols), F32),
        compiler_params=_params(2),
        name="modulation",
    )(cond, w_mod, b_mod.reshape(DEPTH, 1, n_cols))


def _bias_table_kernel(rpb_ref, o_ref):
    l = pl.program_id(0)
    h = pl.program_id(1)
    base = (l * N_HEADS + h) * (RPB_ROWS * RPB_COLS)
    shape = (GRID_W, 2 * GRID_W)
    q = lax.broadcasted_iota(jnp.int32, shape, 0)
    lane = lax.broadcasted_iota(jnp.int32, shape, 1)
    kc = lane & (GRID_W - 1)
    t = kc - q + (WIN_C - 1)
    col_start = jnp.clip(q - WIN_C // 2, 0, GRID_W - WIN_C)
    valid = (kc >= col_start) & (kc < col_start + WIN_C)
    neg_inf = jnp.full(shape, -jnp.inf, F32)
    per_row = []
    for j in range(RPB_ROWS):
        acc = neg_inf
        for m in range(RPB_COLS):
            acc = jnp.where(t == m, rpb_ref[base + j * RPB_COLS + m], acc)
        per_row.append(jnp.where(valid, acc, neg_inf))
    for d in range(WIN_R):
        for pair in range(WIN_R // 2):
            j0 = 2 * pair + WIN_R - 1 - d
            o_ref[0, 0, d, :, pair * 128:(pair + 1) * 128] = jnp.where(
                lane < GRID_W, per_row[j0], per_row[j0 + 1])


def _bias_tables(rpb):
    return pl.pallas_call(
        _bias_table_kernel,
        grid=(DEPTH, N_HEADS),
        in_specs=[pl.BlockSpec(memory_space=pltpu.SMEM)],
        out_specs=pl.BlockSpec((1, 1, WIN_R, GRID_W, WIN_R * GRID_W), lambda l, h: (l, h, 0, 0, 0)),
        out_shape=jax.ShapeDtypeStruct((DEPTH, N_HEADS, WIN_R, GRID_W, WIN_R * GRID_W), F32),
        compiler_params=_params(2),
        name="bias_tables",
    )(rpb.reshape(-1))


def _inproj_kernel(x_ref, mod_ref, w_ref, q_ref, k_ref, v_ref, z_ref, bg_ref, ga_ref, gc_ref,
                   *, mod_row0, tiles_per_seq):
    i = pl.program_id(0)
    row = _cond_row(i, mod_row0, tiles_per_seq)
    sh1 = mod_ref[0, pl.ds(row, 1), 0:D_MODEL]
    sc1 = mod_ref[0, pl.ds(row, 1), D_MODEL:2 * D_MODEL]
    h = (x_ref[...] * (1.0 + sc1) + sh1).astype(BF16)

    def proj(lo, hi):
        return _dot(h, w_ref[:, lo:hi])

    a = ATTN_WIDTH
    c0 = 3 * ATTN_WIDTH
    q = proj(0, a) * SCALE
    k = proj(a, 2 * a)
    v = proj(2 * a, 3 * a)
    for hd in range(N_HEADS):
        sl = slice(hd * HEAD_DIM, (hd + 1) * HEAD_DIM)
        q_ref[0, hd] = q[:, sl].astype(q_ref.dtype)
        k_ref[0, 0, hd] = k[:, sl].astype(k_ref.dtype)
        v_ref[0, 0, hd] = v[:, sl].astype(v_ref.dtype)
    u = proj(c0, c0 + CONV_WIDTH)
    bg_ref[...] = proj(c0 + CONV_WIDTH, c0 + 2 * CONV_WIDTH).astype(BF16)
    cg = proj(c0 + 2 * CONV_WIDTH, c0 + 3 * CONV_WIDTH)
    z_ref[...] = (cg * u).astype(BF16)
    g0 = c0 + 3 * CONV_WIDTH
    ga_ref[...] = proj(g0, g0 + D_MODEL).astype(BF16)
    gc_ref[...] = proj(g0 + D_MODEL, g0 + 2 * D_MODEL).astype(BF16)


def _inproj(x, mod, w_in, layer, *, n_seq, seq_len, mod_row0, kv_prev=None):
    n_tok = n_seq * seq_len
    tps = seq_len // TOKEN_TILE
    n_tiles = n_tok // TOKEN_TILE
    is_ctx = mod_row0 == 0
    head_blk = (1, N_HEADS, TOKEN_TILE, HEAD_DIM)
    kv_blk = (1, 1, N_HEADS, TOKEN_TILE, HEAD_DIM)
    if is_ctx:
        kv_shape = jax.ShapeDtypeStruct((n_seq, DEPTH, N_HEADS, seq_len, HEAD_DIM), F32)
        kv_map = lambda i: (i // tps, layer, 0, i % tps, 0)
    else:
        kv_shape = jax.ShapeDtypeStruct((n_seq, 1, N_HEADS, seq_len, HEAD_DIM), BF16)
        kv_map = lambda i: (i // tps, 0, 0, i % tps, 0)
    tok = lambda width: pl.BlockSpec((TOKEN_TILE, width), lambda i: (i, 0))
    tok_shape = lambda width: jax.ShapeDtypeStruct((n_tok, width), BF16)
    in_specs = [
        tok(D_MODEL),
        pl.BlockSpec((1, COND_ROWS, 6 * D_MODEL), lambda i: (layer, 0, 0)),
        _resident((D_MODEL, IN_COLS)),
    ]
    args = [x, mod, w_in]
    aliases = {}
    if kv_prev is not None:
        in_specs += [pl.BlockSpec(memory_space=pl.ANY)] * 2
        args += list(kv_prev)
        aliases = {3: 1, 4: 2}
        body = lambda x_ref, mod_ref, w_ref, _k_prev, _v_prev, *outs: _inproj_kernel(
            x_ref, mod_ref, w_ref, *outs, mod_row0=mod_row0, tiles_per_seq=tps)
    else:
        body = functools.partial(_inproj_kernel, mod_row0=mod_row0, tiles_per_seq=tps)
    return pl.pallas_call(
        body,
        grid=(n_tiles,),
        in_specs=in_specs,
        out_specs=[
            pl.BlockSpec(head_blk, lambda i: (i // tps, 0, i % tps, 0)),
            pl.BlockSpec(kv_blk, kv_map),
            pl.BlockSpec(kv_blk, kv_map),
            tok(CONV_WIDTH), tok(CONV_WIDTH), tok(D_MODEL), tok(D_MODEL),
        ],
        out_shape=[
            jax.ShapeDtypeStruct((n_seq, N_HEADS, seq_len, HEAD_DIM), BF16),
            kv_shape, kv_shape,
            tok_shape(CONV_WIDTH), tok_shape(CONV_WIDTH), tok_shape(D_MODEL), tok_shape(D_MODEL),
        ],
        input_output_aliases=aliases,
        compiler_params=_params(1),
        name="inproj_ctx" if is_ctx else "inproj_lat",
    )(*args)


def _softmax_parts(s):
    m = jnp.max(s, axis=-1, keepdims=True)
    p = jnp.exp(s - m)
    return p, jnp.sum(p, axis=-1, keepdims=True)


def _ctx_attn_kernel(q_ref, k_ref, v_ref, o_ref):
    for pair in range(N_HEADS // 2):
        outs = []
        for hd in (2 * pair, 2 * pair + 1):
            k = k_ref[0, 0, hd].astype(BF16)
            v = v_ref[0, 0, hd].astype(BF16)
            p, denom = _softmax_parts(_dot_nt(q_ref[0, hd], k))
            outs.append(_dot(p.astype(BF16), v) / denom)
        o_ref[:, pair * 128:(pair + 1) * 128] = jnp.concatenate(outs, axis=-1).astype(BF16)


def _ctx_attention(q, new_k, new_v, layer):
    n_seq, _, seq_len, _ = q.shape
    kv_blk = (1, 1, N_HEADS, seq_len, HEAD_DIM)
    kv_map = lambda b: (b, layer, 0, 0, 0)
    return pl.pallas_call(
        _ctx_attn_kernel,
        grid=(n_seq,),
        in_specs=[
            pl.BlockSpec((1, N_HEADS, seq_len, HEAD_DIM), lambda b: (b, 0, 0, 0)),
            pl.BlockSpec(kv_blk, kv_map),
            pl.BlockSpec(kv_blk, kv_map),
        ],
        out_specs=pl.BlockSpec((seq_len, ATTN_WIDTH), lambda b: (b, 0)),
        out_shape=jax.ShapeDtypeStruct((n_seq * seq_len, ATTN_WIDTH), BF16),
        compiler_params=_params(1),
        name="attn_ctx",
    )(q, new_k, new_v)


HEADS_PER_STEP = 2


def _lat_attn_kernel(q_ref, k_ref, v_ref, kc_ref, vc_ref, tab_ref, o_ref, sctx_ref, *, n_rows):
    band = WIN_R * GRID_W
    for hh in range(HEADS_PER_STEP):
        sctx_ref[hh] = _dot_nt(q_ref[0, hh], kc_ref[0, 0, hh].astype(BF16))

    def chunk(c, carry):
        per_head = []
        for hh in range(HEADS_PER_STEP):
            v_ctx = vc_ref[0, 0, hh].astype(BF16)
            rows_out = []
            for g in range(ROWS_PER_CHUNK):
                r = c * ROWS_PER_CHUNK + g
                first_row = jnp.clip(r - WIN_R // 2, 0, n_rows - WIN_R)
                q0 = pl.multiple_of(r * GRID_W, GRID_W)
                k0 = pl.multiple_of(first_row * GRID_W, GRID_W)
                qr = q_ref[0, hh, pl.ds(q0, GRID_W), :]
                kb = k_ref[0, 0, hh, pl.ds(k0, band), :]
                vb = v_ref[0, 0, hh, pl.ds(k0, band), :]
                s_band = _dot_nt(qr, kb) + tab_ref[0, hh, r - first_row]
                s_ctx = sctx_ref[hh, pl.ds(q0, GRID_W), :]
                m = jnp.maximum(jnp.max(s_band, axis=-1, keepdims=True),
                                jnp.max(s_ctx, axis=-1, keepdims=True))
                p_band = jnp.exp(s_band - m)
                p_ctx = jnp.exp(s_ctx - m)
                denom = (jnp.sum(p_band, axis=-1, keepdims=True)
                         + jnp.sum(p_ctx, axis=-1, keepdims=True))
                o = _dot(p_band.astype(BF16), vb) + _dot(p_ctx.astype(BF16), v_ctx)
                rows_out.append(o / denom)
            per_head.append(jnp.concatenate(rows_out, axis=0))
        t0 = pl.multiple_of(c * (ROWS_PER_CHUNK * GRID_W), ROWS_PER_CHUNK * GRID_W)
        o_ref[0, pl.ds(t0, ROWS_PER_CHUNK * GRID_W), :] = jnp.concatenate(
            per_head, axis=-1).astype(BF16)
        return carry

    lax.fori_loop(0, n_rows // ROWS_PER_CHUNK, chunk, 0)


def _lat_attention(q, k, v, cache_k, cache_v, tables, layer):
    n_seq, _, seq_len, _ = q.shape
    past_len = cache_k.shape[3]
    n_rows = seq_len // GRID_W
    hps = HEADS_PER_STEP
    return pl.pallas_call(
        functools.partial(_lat_attn_kernel, n_rows=n_rows),
        grid=(n_seq, N_HEADS // hps),
        in_specs=[
            pl.BlockSpec((1, hps, seq_len, HEAD_DIM), lambda b, p: (b, p, 0, 0)),
            pl.BlockSpec((1, 1, hps, seq_len, HEAD_DIM), lambda b, p: (b, 0, p, 0, 0)),
            pl.BlockSpec((1, 1, hps, seq_len, HEAD_DIM), lambda b, p: (b, 0, p, 0, 0)),
            pl.BlockSpec((1, 1, hps, past_len, HEAD_DIM), lambda b, p: (b, layer, p, 0, 0)),
            pl.BlockSpec((1, 1, hps, past_len, HEAD_DIM), lambda b, p: (b, layer, p, 0, 0)),
            pl.BlockSpec((1, hps, WIN_R, GRID_W, WIN_R * GRID_W), lambda b, p: (layer, p, 0, 0, 0)),
        ],
        out_specs=pl.BlockSpec((1, seq_len, hps * HEAD_DIM), lambda b, p: (b, 0, p)),
        out_shape=jax.ShapeDtypeStruct((n_seq, seq_len, ATTN_WIDTH), BF16),
        scratch_shapes=[pltpu.VMEM((hps, seq_len, past_len), F32)],
        compiler_params=_params(2),
        name="attn_lat",
    )(q, k, v, cache_k, cache_v, tables)


def _layer_norm(y, g, b):
    mu = jnp.mean(y, axis=-1, keepdims=True)
    yc = y - mu
    var = jnp.mean(yc * yc, axis=-1, keepdims=True)
    return yc * lax.rsqrt(var + LN_EPS) * g + b


def _post_kernel(x_ref, attn_ref, z_ref, zprev_ref, znext_ref, bg_ref, ga_ref, gc_ref, mod_ref,
                 wap_ref, wcp_ref, wo_ref, w1_ref, w2_ref, cw_ref, cb_ref,
                 ln1g_ref, ln1b_ref, b1_ref, b2_ref, ln2g_ref, ln2b_ref, o_ref,
                 *, mod_row0, tiles_per_seq):
    i = pl.program_id(0)
    row = _cond_row(i, mod_row0, tiles_per_seq)
    d = D_MODEL
    g1 = mod_ref[0, pl.ds(row, 1), 2 * d:3 * d]
    sh2 = mod_ref[0, pl.ds(row, 1), 3 * d:4 * d]
    sc2 = mod_ref[0, pl.ds(row, 1), 4 * d:5 * d]
    g2 = mod_ref[0, pl.ds(row, 1), 5 * d:6 * d]

    z = z_ref[...].astype(F32)
    tile_in_seq = i % tiles_per_seq
    prev_row = jnp.where(tile_in_seq == 0, 0.0,
                         zprev_ref[...].astype(F32)[HALO_ROWS - 1:HALO_ROWS, :])
    next_row = jnp.where(tile_in_seq == tiles_per_seq - 1, 0.0,
                         znext_ref[...].astype(F32)[0:1, :])
    pos = lax.broadcasted_iota(jnp.int32, z.shape, 0)
    z_before = jnp.where(pos == 0, prev_row, pltpu.roll(z, 1, 0))
    z_after = jnp.where(pos == TOKEN_TILE - 1, next_row, pltpu.roll(z, TOKEN_TILE - 1, 0))
    conv = z_before * cw_ref[0:1, :] + z * cw_ref[1:2, :] + z_after * cw_ref[2:3, :] + cb_ref[...]
    conv_in = (bg_ref[...].astype(F32) * conv).astype(BF16)

    attn_p = _dot(attn_ref[...], wap_ref[...])
    conv_p = _dot(conv_in, wcp_ref[...])
    merged = (jax.nn.sigmoid(ga_ref[...].astype(F32)) * attn_p
              + jax.nn.sigmoid(gc_ref[...].astype(F32)) * conv_p).astype(BF16)
    mix = _dot(merged, wo_ref[...])
    x1 = _layer_norm(ALPHA * x_ref[...] + g1 * mix, ln1g_ref[...], ln1b_ref[...])

    h2 = (x1 * (1.0 + sc2) + sh2).astype(BF16)
    hidden = jnp.maximum(_dot(h2, w1_ref[...]) + b1_ref[...], 0.0)
    hidden = (hidden * hidden).astype(BF16)
    f = _dot(hidden, w2_ref[...]) + b2_ref[...]
    o_ref[...] = _layer_norm(ALPHA * x1 + g2 * f, ln2g_ref[...], ln2b_ref[...])


def _post(x, attn, z, bg, ga, gc, mod, weights, layer, *, seq_len, mod_row0):
    n_tok = x.shape[0]
    tps = seq_len // TOKEN_TILE
    n_tiles = n_tok // TOKEN_TILE
    halo_per_tile = TOKEN_TILE // HALO_ROWS
    n_halo_blocks = n_tok // HALO_ROWS
    tok = lambda width: pl.BlockSpec((TOKEN_TILE, width), lambda i: (i, 0))
    vec_args = [w[layer].reshape(1, -1) for w in weights["vectors"]]
    mats = [w[layer] for w in weights["matrices"]]
    conv_w = weights["conv_w"][layer]
    in_specs = [
        tok(D_MODEL), tok(ATTN_WIDTH), tok(CONV_WIDTH),
        pl.BlockSpec((HALO_ROWS, CONV_WIDTH), lambda i: (jnp.maximum(i * halo_per_tile - 1, 0), 0)),
        pl.BlockSpec((HALO_ROWS, CONV_WIDTH),
                     lambda i: (jnp.minimum((i + 1) * halo_per_tile, n_halo_blocks - 1), 0)),
        tok(CONV_WIDTH), tok(D_MODEL), tok(D_MODEL),
        pl.BlockSpec((1, COND_ROWS, 6 * D_MODEL), lambda i: (layer, 0, 0)),
    ]
    in_specs += [_resident(m.shape) for m in mats]
    in_specs += [_resident(conv_w.shape)]
    in_specs += [_resident(v.shape) for v in vec_args]
    return pl.pallas_call(
        functools.partial(_post_kernel, mod_row0=mod_row0, tiles_per_seq=tps),
        grid=(n_tiles,),
        in_specs=in_specs,
        out_specs=tok(D_MODEL),
        out_shape=jax.ShapeDtypeStruct((n_tok, D_MODEL), F32),
        compiler_params=_params(1),
        name="post_ctx" if mod_row0 == 0 else "post_lat",
    )(x, attn, z, z, z, bg, ga, gc, mod, *mats, conv_w, *vec_args)


def kernel(x_prompt, x_sample, cache_k, cache_v, c, c_ctx, w_mod, b_mod, w_in, rpb, conv_w, conv_b,
           w_attn_proj, w_conv_proj, w_o, ln1_g, ln1_b, w1, b1, w2, b2, ln2_g, ln2_b):
    batch, seq, _ = x_prompt.shape
    dec_batch, dec_seq, _ = x_sample.shape

    cond = jnp.concatenate(
        [c_ctx[None, :], c, jnp.zeros((COND_ROWS - 1 - dec_batch, D_MODEL), F32)], axis=0)
    mod = _modulation(cond, w_mod, b_mod)
    tables = _bias_tables(rpb)

    w_in_b = w_in.astype(BF16)
    weights = {
        "matrices": [w.astype(BF16) for w in (w_attn_proj, w_conv_proj, w_o, w1, w2)],
        "conv_w": conv_w,
        "vectors": [conv_b, ln1_g, ln1_b, b1, b2, ln2_g, ln2_b],
    }

    xp = x_prompt.reshape(batch * seq, D_MODEL)
    xs = x_sample.reshape(dec_batch * dec_seq, D_MODEL)
    kv = None
    for layer in range(DEPTH):
        q, new_k, new_v, z, bg, ga, gc = _inproj(
            xp, mod, w_in_b[layer], layer, n_seq=batch, seq_len=seq, mod_row0=0, kv_prev=kv)
        kv = (new_k, new_v)
        attn = _ctx_attention(q, new_k, new_v, layer)
        xp = _post(xp, attn, z, bg, ga, gc, mod, weights, layer, seq_len=seq, mod_row0=0)

        q, k, v, z, bg, ga, gc = _inproj(
            xs, mod, w_in_b[layer], layer, n_seq=dec_batch, seq_len=dec_seq, mod_row0=1)
        attn = _lat_attention(q, k, v, cache_k, cache_v, tables, layer)
        xs = _post(xs, attn.reshape(dec_batch * dec_seq, ATTN_WIDTH), z, bg, ga, gc, mod, weights,
                   layer, seq_len=dec_seq, mod_row0=1)

    new_k, new_v = kv
    return (xp.reshape(batch, seq, D_MODEL), xs.reshape(dec_batch, dec_seq, D_MODEL), new_k, new_v)
```

```python
import functools

import jax
import jax.numpy as jnp
from jax import lax
from jax.experimental import pallas as pl
from jax.experimental.pallas import tpu as pltpu

D_MODEL = 1024
DEPTH = 4
N_HEADS = 8
HEAD_DIM = 64
ATTN_WIDTH = N_HEADS * HEAD_DIM
CONV_WIDTH = D_MODEL // 2
D_FF = 4 * D_MODEL
GRID_W = 64
WIN_R = 8
WIN_C = 16
RPB_ROWS = 2 * WIN_R - 1
RPB_COLS = 2 * WIN_C - 1
IN_COLS = 3 * ATTN_WIDTH + 3 * CONV_WIDTH + 2 * D_MODEL
SCALE = HEAD_DIM ** -0.5
ALPHA = (2.0 * DEPTH) ** 0.25
LN_EPS = 1e-5

COND_ROWS = 8
TOKEN_TILE = 512
FF_CHUNK = 1024
POST_ROW_GROUP = 256
HALO_ROWS = 16
ROWS_PER_CHUNK = 4
VMEM_LIMIT_BYTES = 56 * 1024 * 1024

BF16 = jnp.bfloat16
F32 = jnp.float32
NT_DIMS = (((1,), (1,)), ((), ()))


def _dot(a, b):
    return jnp.dot(a, b, preferred_element_type=F32)


def _dot_nt(a, b):
    return lax.dot_general(a, b, NT_DIMS, preferred_element_type=F32)


def _layer_resident(shape, layer):
    zeros = (0,) * len(shape)
    return pl.BlockSpec((None,) + tuple(shape), lambda *_: (layer,) + zeros,
                        pipeline_mode=pl.Buffered(1))


def _cond_row(tile, mod_row0, tiles_per_seq):
    if mod_row0 == 0:
        return 0
    return mod_row0 + tile // tiles_per_seq


def _params(n_grid_dims):
    return pltpu.CompilerParams(
        dimension_semantics=("arbitrary",) * n_grid_dims,
        vmem_limit_bytes=VMEM_LIMIT_BYTES,
    )


MOD_COL_TILE = 1536


def _mod_kernel(cond_ref, w_ref, b_ref, o_ref):
    c = cond_ref[...]
    s = (c * jax.nn.sigmoid(c)).astype(BF16)
    o_ref[0] = _dot(s, w_ref[0].astype(BF16)) + b_ref[0]


def _modulation(cond, w_mod, b_mod):
    n_cols = 6 * D_MODEL
    return pl.pallas_call(
        _mod_kernel,
        grid=(DEPTH, n_cols // MOD_COL_TILE),
        in_specs=[
            pl.BlockSpec((COND_ROWS, D_MODEL), lambda l, j: (0, 0)),
            pl.BlockSpec((1, D_MODEL, MOD_COL_TILE), lambda l, j: (l, 0, j)),
            pl.BlockSpec((1, 1, MOD_COL_TILE), lambda l, j: (l, 0, j)),
        ],
        out_specs=pl.BlockSpec((1, COND_ROWS, MOD_COL_TILE), lambda l, j: (l, 0, j)),
        out_shape=jax.ShapeDtypeStruct((DEPTH, COND_ROWS, n_cols), F32),
        compiler_params=_params(2),
        name="modulation",
    )(cond, w_mod, b_mod.reshape(DEPTH, 1, n_cols))


def _bias_table_kernel(rpb_ref, o_ref):
    l = pl.program_id(0)
    h = pl.program_id(1)
    base = (l * N_HEADS + h) * (RPB_ROWS * RPB_COLS)
    shape = (GRID_W, 2 * GRID_W)
    q = lax.broadcasted_iota(jnp.int32, shape, 0)
    lane = lax.broadcasted_iota(jnp.int32, shape, 1)
    kc = lane & (GRID_W - 1)
    t = kc - q + (WIN_C - 1)
    col_start = jnp.clip(q - WIN_C // 2, 0, GRID_W - WIN_C)
    valid = (kc >= col_start) & (kc < col_start + WIN_C)
    neg_inf = jnp.full(shape, -jnp.inf, F32)
    per_row = []
    for j in range(RPB_ROWS):
        acc = neg_inf
        for m in range(RPB_COLS):
            acc = jnp.where(t == m, rpb_ref[base + j * RPB_COLS + m], acc)
        per_row.append(jnp.where(valid, acc, neg_inf))
    for d in range(WIN_R):
        for pair in range(WIN_R // 2):
            j0 = 2 * pair + WIN_R - 1 - d
            o_ref[0, 0, d, :, pair * 128:(pair + 1) * 128] = jnp.where(
                lane < GRID_W, per_row[j0], per_row[j0 + 1])


def _bias_tables(rpb):
    return pl.pallas_call(
        _bias_table_kernel,
        grid=(DEPTH, N_HEADS),
        in_specs=[pl.BlockSpec(memory_space=pltpu.SMEM)],
        out_specs=pl.BlockSpec((1, 1, WIN_R, GRID_W, WIN_R * GRID_W), lambda l, h: (l, h, 0, 0, 0)),
        out_shape=jax.ShapeDtypeStruct((DEPTH, N_HEADS, WIN_R, GRID_W, WIN_R * GRID_W), F32),
        compiler_params=_params(2),
        name="bias_tables",
    )(rpb.reshape(-1))


def _inproj_kernel(x_ref, mod_ref, w_ref, *rest, mod_row0, tiles_per_seq, seqs_per_tile):
    q_ref, k_ref, v_ref, z_ref, bg_ref, ga_ref, gc_ref = rest[-7:]
    i = pl.program_id(0)
    row = _cond_row(i, mod_row0, tiles_per_seq)
    sh1 = mod_ref[0, pl.ds(row, 1), 0:D_MODEL]
    sc1 = mod_ref[0, pl.ds(row, 1), D_MODEL:2 * D_MODEL]
    h = (x_ref[...] * (1.0 + sc1) + sh1).astype(BF16)

    def proj(lo, hi):
        return _dot(h, w_ref[:, lo:hi])

    a = ATTN_WIDTH
    c0 = 3 * ATTN_WIDTH
    q = proj(0, a) * SCALE
    k = proj(a, 2 * a)
    v = proj(2 * a, 3 * a)
    rows = TOKEN_TILE // seqs_per_tile
    for s in range(seqs_per_tile):
        rs = slice(s * rows, (s + 1) * rows)
        for hd in range(N_HEADS):
            sl = slice(hd * HEAD_DIM, (hd + 1) * HEAD_DIM)
            q_ref[s, hd] = q[rs, sl].astype(q_ref.dtype)
            k_ref[s, 0, hd] = k[rs, sl].astype(k_ref.dtype)
            v_ref[s, 0, hd] = v[rs, sl].astype(v_ref.dtype)
    u = proj(c0, c0 + CONV_WIDTH)
    bg_ref[...] = proj(c0 + CONV_WIDTH, c0 + 2 * CONV_WIDTH).astype(BF16)
    cg = proj(c0 + 2 * CONV_WIDTH, c0 + 3 * CONV_WIDTH)
    z_ref[...] = (cg * u).astype(BF16)
    g0 = c0 + 3 * CONV_WIDTH
    ga_ref[...] = proj(g0, g0 + D_MODEL).astype(BF16)
    gc_ref[...] = proj(g0 + D_MODEL, g0 + 2 * D_MODEL).astype(BF16)


def _inproj(x, mod, w_in, layer, *, n_seq, seq_len, mod_row0, kv_prev=None):
    n_tok = n_seq * seq_len
    n_tiles = n_tok // TOKEN_TILE
    is_ctx = mod_row0 == 0
    if seq_len >= TOKEN_TILE:
        tps, spt, rows = seq_len // TOKEN_TILE, 1, TOKEN_TILE
        head_map = lambda i: (i // tps, 0, i % tps, 0)
        kv_map = lambda i: (i // tps, layer if is_ctx else 0, 0, i % tps, 0)
    else:
        tps, spt, rows = 1, TOKEN_TILE // seq_len, seq_len
        head_map = lambda i: (i, 0, 0, 0)
        kv_map = lambda i: (i, layer if is_ctx else 0, 0, 0, 0)
    head_blk = (spt, N_HEADS, rows, HEAD_DIM)
    kv_blk = (spt, 1, N_HEADS, rows, HEAD_DIM)
    if is_ctx:
        kv_shape = jax.ShapeDtypeStruct((n_seq, DEPTH, N_HEADS, seq_len, HEAD_DIM), F32)
    else:
        kv_shape = jax.ShapeDtypeStruct((n_seq, 1, N_HEADS, seq_len, HEAD_DIM), BF16)
    tok = lambda width: pl.BlockSpec((TOKEN_TILE, width), lambda i: (i, 0))
    tok_shape = lambda width: jax.ShapeDtypeStruct((n_tok, width), BF16)
    in_specs = [
        tok(D_MODEL),
        pl.BlockSpec((1, COND_ROWS, 6 * D_MODEL), lambda i: (layer, 0, 0)),
        _layer_resident((D_MODEL, IN_COLS), layer),
    ]
    args = [x, mod, w_in]
    aliases = {}
    if kv_prev is not None:
        in_specs += [pl.BlockSpec(memory_space=pl.ANY)] * 2
        args += list(kv_prev)
        aliases = {3: 1, 4: 2}
    return pl.pallas_call(
        functools.partial(_inproj_kernel, mod_row0=mod_row0, tiles_per_seq=tps, seqs_per_tile=spt),
        grid=(n_tiles,),
        in_specs=in_specs,
        out_specs=[
            pl.BlockSpec(head_blk, head_map),
            pl.BlockSpec(kv_blk, kv_map),
            pl.BlockSpec(kv_blk, kv_map),
            tok(CONV_WIDTH), tok(CONV_WIDTH), tok(D_MODEL), tok(D_MODEL),
        ],
        out_shape=[
            jax.ShapeDtypeStruct((n_seq, N_HEADS, seq_len, HEAD_DIM), BF16),
            kv_shape, kv_shape,
            tok_shape(CONV_WIDTH), tok_shape(CONV_WIDTH), tok_shape(D_MODEL), tok_shape(D_MODEL),
        ],
        input_output_aliases=aliases,
        compiler_params=_params(1),
        name="inproj_ctx" if is_ctx else "inproj_lat",
    )(*args)


def _softmax_parts(s):
    m = jnp.max(s, axis=-1, keepdims=True)
    p = jnp.exp(s - m)
    return p, jnp.sum(p, axis=-1, keepdims=True)


def _ctx_attn_kernel(q_ref, k_ref, v_ref, o_ref):
    for pair in range(N_HEADS // 2):
        outs = []
        for hd in (2 * pair, 2 * pair + 1):
            k = k_ref[0, 0, hd].astype(BF16)
            v = v_ref[0, 0, hd].astype(BF16)
            p, denom = _softmax_parts(_dot_nt(q_ref[0, hd], k))
            outs.append(_dot(p.astype(BF16), v) / denom)
        o_ref[:, pair * 128:(pair + 1) * 128] = jnp.concatenate(outs, axis=-1).astype(BF16)


def _ctx_attention(q, new_k, new_v, layer):
    n_seq, _, seq_len, _ = q.shape
    kv_blk = (1, 1, N_HEADS, seq_len, HEAD_DIM)
    kv_map = lambda b: (b, layer, 0, 0, 0)
    return pl.pallas_call(
        _ctx_attn_kernel,
        grid=(n_seq,),
        in_specs=[
            pl.BlockSpec((1, N_HEADS, seq_len, HEAD_DIM), lambda b: (b, 0, 0, 0)),
            pl.BlockSpec(kv_blk, kv_map),
            pl.BlockSpec(kv_blk, kv_map),
        ],
        out_specs=pl.BlockSpec((seq_len, ATTN_WIDTH), lambda b: (b, 0)),
        out_shape=jax.ShapeDtypeStruct((n_seq * seq_len, ATTN_WIDTH), BF16),
        compiler_params=_params(1),
        name="attn_ctx",
    )(q, new_k, new_v)


HEADS_PER_STEP = 2


def _lat_attn_kernel(q_ref, k_ref, v_ref, kc_ref, vc_ref, tab_ref, o_ref, sctx_ref, *, n_rows):
    band = WIN_R * GRID_W
    for hh in range(HEADS_PER_STEP):
        sctx_ref[hh] = _dot_nt(q_ref[0, hh], kc_ref[0, 0, hh].astype(BF16))

    def chunk(c, carry):
        per_head = []
        for hh in range(HEADS_PER_STEP):
            v_ctx = vc_ref[0, 0, hh].astype(BF16)
            rows_out = []
            for g in range(ROWS_PER_CHUNK):
                r = c * ROWS_PER_CHUNK + g
                first_row = jnp.clip(r - WIN_R // 2, 0, n_rows - WIN_R)
                q0 = pl.multiple_of(r * GRID_W, GRID_W)
                k0 = pl.multiple_of(first_row * GRID_W, GRID_W)
                qr = q_ref[0, hh, pl.ds(q0, GRID_W), :]
                kb = k_ref[0, 0, hh, pl.ds(k0, band), :]
                vb = v_ref[0, 0, hh, pl.ds(k0, band), :]
                s_band = _dot_nt(qr, kb) + tab_ref[0, hh, r - first_row]
                s_ctx = sctx_ref[hh, pl.ds(q0, GRID_W), :]
                m = jnp.maximum(jnp.max(s_band, axis=-1, keepdims=True),
                                jnp.max(s_ctx, axis=-1, keepdims=True))
                p_band = jnp.exp(s_band - m)
                p_ctx = jnp.exp(s_ctx - m)
                denom = (jnp.sum(p_band, axis=-1, keepdims=True)
                         + jnp.sum(p_ctx, axis=-1, keepdims=True))
                o = _dot(p_band.astype(BF16), vb) + _dot(p_ctx.astype(BF16), v_ctx)
                rows_out.append(o / denom)
            per_head.append(jnp.concatenate(rows_out, axis=0))
        t0 = pl.multiple_of(c * (ROWS_PER_CHUNK * GRID_W), ROWS_PER_CHUNK * GRID_W)
        o_ref[0, pl.ds(t0, ROWS_PER_CHUNK * GRID_W), :] = jnp.concatenate(
            per_head, axis=-1).astype(BF16)
        return carry

    lax.fori_loop(0, n_rows // ROWS_PER_CHUNK, chunk, 0)


def _lat_attention(q, k, v, cache_k, cache_v, tables, layer):
    n_seq, _, seq_len, _ = q.shape
    past_len = cache_k.shape[3]
    n_rows = seq_len // GRID_W
    hps = HEADS_PER_STEP
    return pl.pallas_call(
        functools.partial(_lat_attn_kernel, n_rows=n_rows),
        grid=(n_seq, N_HEADS // hps),
        in_specs=[
            pl.BlockSpec((1, hps, seq_len, HEAD_DIM), lambda b, p: (b, p, 0, 0)),
            pl.BlockSpec((1, 1, hps, seq_len, HEAD_DIM), lambda b, p: (b, 0, p, 0, 0)),
            pl.BlockSpec((1, 1, hps, seq_len, HEAD_DIM), lambda b, p: (b, 0, p, 0, 0)),
            pl.BlockSpec((1, 1, hps, past_len, HEAD_DIM), lambda b, p: (b, layer, p, 0, 0)),
            pl.BlockSpec((1, 1, hps, past_len, HEAD_DIM), lambda b, p: (b, layer, p, 0, 0)),
            pl.BlockSpec((1, hps, WIN_R, GRID_W, WIN_R * GRID_W), lambda b, p: (layer, p, 0, 0, 0)),
        ],
        out_specs=pl.BlockSpec((1, seq_len, hps * HEAD_DIM), lambda b, p: (b, 0, p)),
        out_shape=jax.ShapeDtypeStruct((n_seq, seq_len, ATTN_WIDTH), BF16),
        scratch_shapes=[pltpu.VMEM((hps, seq_len, past_len), F32)],
        compiler_params=_params(2),
        name="attn_lat",
    )(q, k, v, cache_k, cache_v, tables)


def _layer_norm(y, g, b):
    mu = jnp.mean(y, axis=-1, keepdims=True)
    yc = y - mu
    var = jnp.mean(yc * yc, axis=-1, keepdims=True)
    return yc * lax.rsqrt(var + LN_EPS) * g + b


def _post_kernel(x_ref, attn_ref, z_ref, zprev_ref, znext_ref, bg_ref, ga_ref, gc_ref, mod_ref,
                 wap_ref, wcp_ref, wo_ref, w1_ref, w2_ref, cw_ref, cb_ref,
                 ln1g_ref, ln1b_ref, b1_ref, b2_ref, ln2g_ref, ln2b_ref, o_ref,
                 *, mod_row0, tiles_per_seq, seq_len):
    i = pl.program_id(0)
    row = _cond_row(i, mod_row0, tiles_per_seq)
    d = D_MODEL
    g1 = mod_ref[0, pl.ds(row, 1), 2 * d:3 * d]
    sh2 = mod_ref[0, pl.ds(row, 1), 3 * d:4 * d]
    sc2 = mod_ref[0, pl.ds(row, 1), 4 * d:5 * d]
    g2 = mod_ref[0, pl.ds(row, 1), 5 * d:6 * d]

    z = z_ref[...].astype(F32)
    prev_row = zprev_ref[...].astype(F32)[HALO_ROWS - 1:HALO_ROWS, :]
    next_row = znext_ref[...].astype(F32)[0:1, :]
    pos = lax.broadcasted_iota(jnp.int32, z.shape, 0)
    seq_pos = (i * TOKEN_TILE + pos) & (seq_len - 1)
    z_before = jnp.where(pos == 0, prev_row, pltpu.roll(z, 1, 0))
    z_before = jnp.where(seq_pos == 0, 0.0, z_before)
    z_after = jnp.where(pos == TOKEN_TILE - 1, next_row, pltpu.roll(z, TOKEN_TILE - 1, 0))
    z_after = jnp.where(seq_pos == seq_len - 1, 0.0, z_after)
    conv = z_before * cw_ref[0:1, :] + z * cw_ref[1:2, :] + z_after * cw_ref[2:3, :] + cb_ref[...]
    conv_in = (bg_ref[...].astype(F32) * conv).astype(BF16)

    def merge(rs):
        attn_p = _dot(attn_ref[rs, :], wap_ref[...])
        conv_p = _dot(conv_in[rs, :], wcp_ref[...])
        return (jax.nn.sigmoid(ga_ref[rs, :].astype(F32)) * attn_p
                + jax.nn.sigmoid(gc_ref[rs, :].astype(F32)) * conv_p).astype(BF16)

    def mix_norm(merged, rs):
        mix = _dot(merged, wo_ref[...])
        x1 = _layer_norm(ALPHA * x_ref[rs, :] + g1 * mix, ln1g_ref[...], ln1b_ref[...])
        return x1, (x1 * (1.0 + sc2) + sh2).astype(BF16)

    def mlp_chunk(h2, c):
        cols = slice(c * FF_CHUNK, (c + 1) * FF_CHUNK)
        hidden = jnp.maximum(_dot(h2, w1_ref[:, cols]) + b1_ref[:, cols], 0.0)
        return _dot((hidden * hidden).astype(BF16), w2_ref[cols, :])

    def finish(x1, f, rs):
        o_ref[rs, :] = _layer_norm(ALPHA * x1 + g2 * f, ln2g_ref[...], ln2b_ref[...])

    groups = [slice(r0, r0 + POST_ROW_GROUP) for r0 in range(0, TOKEN_TILE, POST_ROW_GROUP)]
    n_chunks = D_FF // FF_CHUNK
    x1, h2 = mix_norm(merge(groups[0]), groups[0])
    f = b2_ref[...] + mlp_chunk(h2, 0)
    for n, rs in enumerate(groups):
        nxt = groups[n + 1] if n + 1 < len(groups) else None
        if nxt is not None:
            merged_next = merge(nxt)
        f = f + mlp_chunk(h2, 1)
        if nxt is not None:
            x1_next, h2_next = mix_norm(merged_next, nxt)
        for c in range(2, n_chunks):
            f = f + mlp_chunk(h2, c)
        if nxt is not None:
            f_next = b2_ref[...] + mlp_chunk(h2_next, 0)
        finish(x1, f, rs)
        if nxt is not None:
            x1, h2, f = x1_next, h2_next, f_next


def _post(x, attn, z, bg, ga, gc, mod, weights, layer, *, seq_len, mod_row0):
    n_tok = x.shape[0]
    assert seq_len & (seq_len - 1) == 0
    tps = max(seq_len // TOKEN_TILE, 1)
    n_tiles = n_tok // TOKEN_TILE
    halo_per_tile = TOKEN_TILE // HALO_ROWS
    n_halo_blocks = n_tok // HALO_ROWS
    tok = lambda width: pl.BlockSpec((TOKEN_TILE, width), lambda i: (i, 0))
    in_specs = [
        tok(D_MODEL), tok(ATTN_WIDTH), tok(CONV_WIDTH),
        pl.BlockSpec((HALO_ROWS, CONV_WIDTH), lambda i: (jnp.maximum(i * halo_per_tile - 1, 0), 0)),
        pl.BlockSpec((HALO_ROWS, CONV_WIDTH),
                     lambda i: (jnp.minimum((i + 1) * halo_per_tile, n_halo_blocks - 1), 0)),
        tok(CONV_WIDTH), tok(D_MODEL), tok(D_MODEL),
        pl.BlockSpec((1, COND_ROWS, 6 * D_MODEL), lambda i: (layer, 0, 0)),
    ]
    per_layer = weights["matrices"] + [weights["conv_w"]] + weights["vectors"]
    in_specs += [_layer_resident(w.shape[1:], layer) for w in per_layer]
    return pl.pallas_call(
        functools.partial(_post_kernel, mod_row0=mod_row0, tiles_per_seq=tps, seq_len=seq_len),
        grid=(n_tiles,),
        in_specs=in_specs,
        out_specs=tok(D_MODEL),
        out_shape=jax.ShapeDtypeStruct((n_tok, D_MODEL), F32),
        compiler_params=_params(1),
        name="post_ctx" if mod_row0 == 0 else "post_lat",
    )(x, attn, z, z, z, bg, ga, gc, mod, *per_layer)


def kernel(x_prompt, x_sample, cache_k, cache_v, c, c_ctx, w_mod, b_mod, w_in, rpb, conv_w, conv_b,
           w_attn_proj, w_conv_proj, w_o, ln1_g, ln1_b, w1, b1, w2, b2, ln2_g, ln2_b):
    batch, seq, _ = x_prompt.shape
    dec_batch, dec_seq, _ = x_sample.shape

    cond = jnp.concatenate(
        [c_ctx[None, :], c, jnp.zeros((COND_ROWS - 1 - dec_batch, D_MODEL), F32)], axis=0)
    mod = _modulation(cond, w_mod, b_mod)
    tables = _bias_tables(rpb)

    w_in_b = w_in.astype(BF16)
    weights = {
        "matrices": [w.astype(BF16) for w in (w_attn_proj, w_conv_proj, w_o, w1, w2)],
        "conv_w": conv_w,
        "vectors": [v.reshape(DEPTH, 1, -1) for v in (conv_b, ln1_g, ln1_b, b1, b2, ln2_g, ln2_b)],
    }

    xp = x_prompt.reshape(batch * seq, D_MODEL)
    xs = x_sample.reshape(dec_batch * dec_seq, D_MODEL)
    kv = None
    for layer in range(DEPTH):
        q, new_k, new_v, z, bg, ga, gc = _inproj(
            xp, mod, w_in_b, layer, n_seq=batch, seq_len=seq, mod_row0=0, kv_prev=kv)
        kv = (new_k, new_v)
        attn = _ctx_attention(q, new_k, new_v, layer)
        xp = _post(xp, attn, z, bg, ga, gc, mod, weights, layer, seq_len=seq, mod_row0=0)

        q, k, v, z, bg, ga, gc = _inproj(
            xs, mod, w_in_b, layer, n_seq=dec_batch, seq_len=dec_seq, mod_row0=1)
        attn = _lat_attention(q, k, v, cache_k, cache_v, tables, layer)
        xs = _post(xs, attn.reshape(dec_batch * dec_seq, ATTN_WIDTH), z, bg, ga, gc, mod, weights,
                   layer, seq_len=dec_seq, mod_row0=1)

    new_k, new_v = kv
    return (xp.reshape(batch, seq, D_MODEL), xs.reshape(dec_batch, dec_seq, D_MODEL), new_k, new_v)
```

```python
import functools

import jax
import jax.numpy as jnp
from jax import lax
from jax.experimental import pallas as pl
from jax.experimental.pallas import tpu as pltpu

D_MODEL = 1024
DEPTH = 4
N_HEADS = 8
HEAD_DIM = 64
ATTN_WIDTH = N_HEADS * HEAD_DIM
CONV_WIDTH = D_MODEL // 2
D_FF = 4 * D_MODEL
GRID_W = 64
WIN_R = 8
WIN_C = 16
RPB_ROWS = 2 * WIN_R - 1
RPB_COLS = 2 * WIN_C - 1
IN_COLS = 3 * ATTN_WIDTH + 3 * CONV_WIDTH + 2 * D_MODEL
SCALE = HEAD_DIM ** -0.5
ALPHA = (2.0 * DEPTH) ** 0.25
LN_EPS = 1e-5

COND_ROWS = 8
TOKEN_TILE = 512
FF_CHUNK = 1024
POST_ROW_GROUP = 256
HALO_ROWS = 16
ROWS_PER_CHUNK = 4
VMEM_LIMIT_BYTES = 56 * 1024 * 1024

BF16 = jnp.bfloat16
F32 = jnp.float32
NT_DIMS = (((1,), (1,)), ((), ()))


def _dot(a, b):
    return jnp.dot(a, b, preferred_element_type=F32)


def _dot_nt(a, b):
    return lax.dot_general(a, b, NT_DIMS, preferred_element_type=F32)


def _layer_resident(shape, layer):
    zeros = (0,) * len(shape)
    return pl.BlockSpec((None,) + tuple(shape), lambda *_: (layer,) + zeros,
                        pipeline_mode=pl.Buffered(1))


def _cond_row(tile, mod_row0, tiles_per_seq):
    if mod_row0 == 0:
        return 0
    return mod_row0 + tile // tiles_per_seq


def _params(n_grid_dims):
    return pltpu.CompilerParams(
        dimension_semantics=("arbitrary",) * n_grid_dims,
        vmem_limit_bytes=VMEM_LIMIT_BYTES,
    )


MOD_COL_TILE = 1536


def _mod_kernel(cond_ref, w_ref, b_ref, o_ref):
    c = cond_ref[...]
    s = (c * jax.nn.sigmoid(c)).astype(BF16)
    o_ref[0] = _dot(s, w_ref[0].astype(BF16)) + b_ref[0]


def _modulation(cond, w_mod, b_mod):
    n_cols = 6 * D_MODEL
    return pl.pallas_call(
        _mod_kernel,
        grid=(DEPTH, n_cols // MOD_COL_TILE),
        in_specs=[
            pl.BlockSpec((COND_ROWS, D_MODEL), lambda l, j: (0, 0)),
            pl.BlockSpec((1, D_MODEL, MOD_COL_TILE), lambda l, j: (l, 0, j)),
            pl.BlockSpec((1, 1, MOD_COL_TILE), lambda l, j: (l, 0, j)),
        ],
        out_specs=pl.BlockSpec((1, COND_ROWS, MOD_COL_TILE), lambda l, j: (l, 0, j)),
        out_shape=jax.ShapeDtypeStruct((DEPTH, COND_ROWS, n_cols), F32),
        compiler_params=_params(2),
        name="modulation",
    )(cond, w_mod, b_mod.reshape(DEPTH, 1, n_cols))


UNION_ROWS = WIN_R + ROWS_PER_CHUNK
N_CHUNK_KINDS = 3
assert ROWS_PER_CHUNK == WIN_R // 2


def _bias_table_kernel(rpb_ref, o_ref):
    l = pl.program_id(0)
    h = pl.program_id(1)
    base = (l * N_HEADS + h) * (RPB_ROWS * RPB_COLS)
    shape = (GRID_W, 2 * GRID_W)
    q = lax.broadcasted_iota(jnp.int32, shape, 0)
    lane = lax.broadcasted_iota(jnp.int32, shape, 1)
    kc = lane & (GRID_W - 1)
    t = kc - q + (WIN_C - 1)
    col_start = jnp.clip(q - WIN_C // 2, 0, GRID_W - WIN_C)
    valid = (kc >= col_start) & (kc < col_start + WIN_C)
    neg_inf = jnp.full(shape, -jnp.inf, F32)
    per_row = []
    for j in range(RPB_ROWS):
        acc = neg_inf
        for m in range(RPB_COLS):
            acc = jnp.where(t == m, rpb_ref[base + j * RPB_COLS + m], acc)
        per_row.append(jnp.where(valid, acc, neg_inf))
    chunk_kinds = (
        [(0, g) for g in range(ROWS_PER_CHUNK)],
        [(g, WIN_R // 2) for g in range(ROWS_PER_CHUNK)],
        [(ROWS_PER_CHUNK, WIN_R // 2 + g) for g in range(ROWS_PER_CHUNK)],
    )
    for kind, rows in enumerate(chunk_kinds):
        for g, (offset, d) in enumerate(rows):
            for pair in range(UNION_ROWS // 2):
                halves = []
                for j in (2 * pair, 2 * pair + 1):
                    i = j - offset
                    halves.append(per_row[i + WIN_R - 1 - d] if 0 <= i < WIN_R else neg_inf)
                o_ref[0, 0, kind, g * GRID_W:(g + 1) * GRID_W, pair * 128:(pair + 1) * 128] = (
                    jnp.where(lane < GRID_W, halves[0], halves[1]))


def _bias_tables(rpb):
    blk = (1, 1, N_CHUNK_KINDS, ROWS_PER_CHUNK * GRID_W, UNION_ROWS * GRID_W)
    return pl.pallas_call(
        _bias_table_kernel,
        grid=(DEPTH, N_HEADS),
        in_specs=[pl.BlockSpec(memory_space=pltpu.SMEM)],
        out_specs=pl.BlockSpec(blk, lambda l, h: (l, h, 0, 0, 0)),
        out_shape=jax.ShapeDtypeStruct((DEPTH, N_HEADS) + blk[2:], F32),
        compiler_params=_params(2),
        name="bias_tables",
    )(rpb.reshape(-1))


def _inproj_kernel(x_ref, mod_ref, w_ref, *rest, mod_row0, tiles_per_seq, seqs_per_tile):
    q_ref, k_ref, v_ref, z_ref, bg_ref, ga_ref, gc_ref = rest[-7:]
    i = pl.program_id(0)
    row = _cond_row(i, mod_row0, tiles_per_seq)
    sh1 = mod_ref[0, pl.ds(row, 1), 0:D_MODEL]
    sc1 = mod_ref[0, pl.ds(row, 1), D_MODEL:2 * D_MODEL]
    h = (x_ref[...] * (1.0 + sc1) + sh1).astype(BF16)

    def proj(lo, hi):
        return _dot(h, w_ref[:, lo:hi])

    a = ATTN_WIDTH
    c0 = 3 * ATTN_WIDTH
    q = proj(0, a) * SCALE
    k = proj(a, 2 * a)
    v = proj(2 * a, 3 * a)
    rows = TOKEN_TILE // seqs_per_tile
    for s in range(seqs_per_tile):
        rs = slice(s * rows, (s + 1) * rows)
        for hd in range(N_HEADS):
            sl = slice(hd * HEAD_DIM, (hd + 1) * HEAD_DIM)
            q_ref[s, hd] = q[rs, sl].astype(q_ref.dtype)
            k_ref[s, 0, hd] = k[rs, sl].astype(k_ref.dtype)
            v_ref[s, 0, hd] = v[rs, sl].astype(v_ref.dtype)
    u = proj(c0, c0 + CONV_WIDTH)
    bg_ref[...] = proj(c0 + CONV_WIDTH, c0 + 2 * CONV_WIDTH).astype(BF16)
    cg = proj(c0 + 2 * CONV_WIDTH, c0 + 3 * CONV_WIDTH)
    z_ref[...] = (cg * u).astype(BF16)
    g0 = c0 + 3 * CONV_WIDTH
    ga_ref[...] = proj(g0, g0 + D_MODEL).astype(BF16)
    gc_ref[...] = proj(g0 + D_MODEL, g0 + 2 * D_MODEL).astype(BF16)


def _inproj(x, mod, w_in, layer, *, n_seq, seq_len, mod_row0, kv_prev=None):
    n_tok = n_seq * seq_len
    n_tiles = n_tok // TOKEN_TILE
    is_ctx = mod_row0 == 0
    if seq_len >= TOKEN_TILE:
        tps, spt, rows = seq_len // TOKEN_TILE, 1, TOKEN_TILE
        head_map = lambda i: (i // tps, 0, i % tps, 0)
        kv_map = lambda i: (i // tps, layer if is_ctx else 0, 0, i % tps, 0)
    else:
        tps, spt, rows = 1, TOKEN_TILE // seq_len, seq_len
        head_map = lambda i: (i, 0, 0, 0)
        kv_map = lambda i: (i, layer if is_ctx else 0, 0, 0, 0)
    head_blk = (spt, N_HEADS, rows, HEAD_DIM)
    kv_blk = (spt, 1, N_HEADS, rows, HEAD_DIM)
    if is_ctx:
        kv_shape = jax.ShapeDtypeStruct((n_seq, DEPTH, N_HEADS, seq_len, HEAD_DIM), F32)
    else:
        kv_shape = jax.ShapeDtypeStruct((n_seq, 1, N_HEADS, seq_len, HEAD_DIM), BF16)
    tok = lambda width: pl.BlockSpec((TOKEN_TILE, width), lambda i: (i, 0))
    tok_shape = lambda width: jax.ShapeDtypeStruct((n_tok, width), BF16)
    in_specs = [
        tok(D_MODEL),
        pl.BlockSpec((1, COND_ROWS, 6 * D_MODEL), lambda i: (layer, 0, 0)),
        _layer_resident((D_MODEL, IN_COLS), layer),
    ]
    args = [x, mod, w_in]
    aliases = {}
    if kv_prev is not None:
        in_specs += [pl.BlockSpec(memory_space=pl.ANY)] * 2
        args += list(kv_prev)
        aliases = {3: 1, 4: 2}
    return pl.pallas_call(
        functools.partial(_inproj_kernel, mod_row0=mod_row0, tiles_per_seq=tps, seqs_per_tile=spt),
        grid=(n_tiles,),
        in_specs=in_specs,
        out_specs=[
            pl.BlockSpec(head_blk, head_map),
            pl.BlockSpec(kv_blk, kv_map),
            pl.BlockSpec(kv_blk, kv_map),
            tok(CONV_WIDTH), tok(CONV_WIDTH), tok(D_MODEL), tok(D_MODEL),
        ],
        out_shape=[
            jax.ShapeDtypeStruct((n_seq, N_HEADS, seq_len, HEAD_DIM), BF16),
            kv_shape, kv_shape,
            tok_shape(CONV_WIDTH), tok_shape(CONV_WIDTH), tok_shape(D_MODEL), tok_shape(D_MODEL),
        ],
        input_output_aliases=aliases,
        compiler_params=_params(1),
        name="inproj_ctx" if is_ctx else "inproj_lat",
    )(*args)


def _softmax_parts(s):
    m = jnp.max(s, axis=-1, keepdims=True)
    p = jnp.exp(s - m)
    return p, jnp.sum(p, axis=-1, keepdims=True)


CTX_SEQS_PER_STEP = 4


def _ctx_attn_kernel(q_ref, k_ref, v_ref, o_ref):
    seq_len = q_ref.shape[2]
    for s in range(CTX_SEQS_PER_STEP):
        for pair in range(N_HEADS // 2):
            outs = []
            for hd in (2 * pair, 2 * pair + 1):
                k = k_ref[s, 0, hd].astype(BF16)
                v = v_ref[s, 0, hd].astype(BF16)
                p, denom = _softmax_parts(_dot_nt(q_ref[s, hd], k))
                outs.append(_dot(p.astype(BF16), v) / denom)
            o_ref[s * seq_len:(s + 1) * seq_len, pair * 128:(pair + 1) * 128] = (
                jnp.concatenate(outs, axis=-1).astype(BF16))


def _ctx_attention(q, new_k, new_v, layer):
    n_seq, _, seq_len, _ = q.shape
    sps = CTX_SEQS_PER_STEP
    kv_blk = (sps, 1, N_HEADS, seq_len, HEAD_DIM)
    kv_map = lambda b: (b, layer, 0, 0, 0)
    return pl.pallas_call(
        _ctx_attn_kernel,
        grid=(n_seq // sps,),
        in_specs=[
            pl.BlockSpec((sps, N_HEADS, seq_len, HEAD_DIM), lambda b: (b, 0, 0, 0)),
            pl.BlockSpec(kv_blk, kv_map),
            pl.BlockSpec(kv_blk, kv_map),
        ],
        out_specs=pl.BlockSpec((sps * seq_len, ATTN_WIDTH), lambda b: (b, 0)),
        out_shape=jax.ShapeDtypeStruct((n_seq * seq_len, ATTN_WIDTH), BF16),
        compiler_params=_params(1),
        name="attn_ctx",
    )(q, new_k, new_v)


HEADS_PER_STEP = 2


def _lat_attn_kernel(q_ref, k_ref, v_ref, kc_ref, vc_ref, tab_ref, o_ref, *, n_rows):
    n_chunks = n_rows // ROWS_PER_CHUNK
    chunk_q = ROWS_PER_CHUNK * GRID_W
    union = UNION_ROWS * GRID_W
    k_ctx = [kc_ref[0, 0, hh].astype(BF16) for hh in range(HEADS_PER_STEP)]
    v_ctx = [vc_ref[0, 0, hh].astype(BF16) for hh in range(HEADS_PER_STEP)]

    def chunk(c, carry):
        first_key_row = jnp.clip(c * ROWS_PER_CHUNK - WIN_R // 2, 0, n_rows - UNION_ROWS)
        q0 = pl.multiple_of(c * chunk_q, chunk_q)
        k0 = pl.multiple_of(first_key_row * GRID_W, GRID_W)
        kind = jnp.where(c == 0, 0, jnp.where(c == n_chunks - 1, 2, 1))
        per_head = []
        for hh in range(HEADS_PER_STEP):
            qc = q_ref[0, hh, pl.ds(q0, chunk_q), :]
            kb = k_ref[0, 0, hh, pl.ds(k0, union), :]
            vb = v_ref[0, 0, hh, pl.ds(k0, union), :]
            s_band = _dot_nt(qc, kb) + tab_ref[0, hh, kind]
            s_ctx = _dot_nt(qc, k_ctx[hh])
            m = jnp.maximum(jnp.max(s_band, axis=-1, keepdims=True),
                            jnp.max(s_ctx, axis=-1, keepdims=True))
            p_band = jnp.exp(s_band - m)
            p_ctx = jnp.exp(s_ctx - m)
            denom = (jnp.sum(p_band, axis=-1, keepdims=True)
                     + jnp.sum(p_ctx, axis=-1, keepdims=True))
            o = _dot(p_band.astype(BF16), vb) + _dot(p_ctx.astype(BF16), v_ctx[hh])
            per_head.append(o / denom)
        o_ref[0, pl.ds(q0, chunk_q), :] = jnp.concatenate(per_head, axis=-1).astype(BF16)
        return carry

    lax.fori_loop(0, n_chunks, chunk, 0)


def _lat_attention(q, k, v, cache_k, cache_v, tables, layer):
    n_seq, _, seq_len, _ = q.shape
    past_len = cache_k.shape[3]
    n_rows = seq_len // GRID_W
    assert n_rows % ROWS_PER_CHUNK == 0 and n_rows >= 2 * UNION_ROWS
    hps = HEADS_PER_STEP
    return pl.pallas_call(
        functools.partial(_lat_attn_kernel, n_rows=n_rows),
        grid=(N_HEADS // hps, n_seq),
        in_specs=[
            pl.BlockSpec((1, hps, seq_len, HEAD_DIM), lambda p, b: (b, p, 0, 0)),
            pl.BlockSpec((1, 1, hps, seq_len, HEAD_DIM), lambda p, b: (b, 0, p, 0, 0)),
            pl.BlockSpec((1, 1, hps, seq_len, HEAD_DIM), lambda p, b: (b, 0, p, 0, 0)),
            pl.BlockSpec((1, 1, hps, past_len, HEAD_DIM), lambda p, b: (b, layer, p, 0, 0)),
            pl.BlockSpec((1, 1, hps, past_len, HEAD_DIM), lambda p, b: (b, layer, p, 0, 0)),
            pl.BlockSpec((1, hps) + tables.shape[2:], lambda p, b: (layer, p, 0, 0, 0)),
        ],
        out_specs=pl.BlockSpec((1, seq_len, hps * HEAD_DIM), lambda p, b: (b, 0, p)),
        out_shape=jax.ShapeDtypeStruct((n_seq, seq_len, ATTN_WIDTH), BF16),
        compiler_params=_params(2),
        name="attn_lat",
    )(q, k, v, cache_k, cache_v, tables)


def _layer_norm(y, g, b):
    mu = jnp.mean(y, axis=-1, keepdims=True)
    yc = y - mu
    var = jnp.mean(yc * yc, axis=-1, keepdims=True)
    return yc * lax.rsqrt(var + LN_EPS) * g + b


def _post_kernel(x_ref, attn_ref, z_ref, zprev_ref, znext_ref, bg_ref, ga_ref, gc_ref, mod_ref,
                 wap_ref, wcp_ref, wo_ref, w1_ref, w2_ref, cw_ref, cb_ref,
                 ln1g_ref, ln1b_ref, b1_ref, b2_ref, ln2g_ref, ln2b_ref, o_ref,
                 *, mod_row0, tiles_per_seq, seq_len):
    i = pl.program_id(0)
    row = _cond_row(i, mod_row0, tiles_per_seq)
    d = D_MODEL
    g1 = mod_ref[0, pl.ds(row, 1), 2 * d:3 * d]
    sh2 = mod_ref[0, pl.ds(row, 1), 3 * d:4 * d]
    sc2 = mod_ref[0, pl.ds(row, 1), 4 * d:5 * d]
    g2 = mod_ref[0, pl.ds(row, 1), 5 * d:6 * d]

    z = z_ref[...].astype(F32)
    prev_row = zprev_ref[...].astype(F32)[HALO_ROWS - 1:HALO_ROWS, :]
    next_row = znext_ref[...].astype(F32)[0:1, :]
    pos = lax.broadcasted_iota(jnp.int32, z.shape, 0)
    seq_pos = (i * TOKEN_TILE + pos) & (seq_len - 1)
    z_before = jnp.where(pos == 0, prev_row, pltpu.roll(z, 1, 0))
    z_before = jnp.where(seq_pos == 0, 0.0, z_before)
    z_after = jnp.where(pos == TOKEN_TILE - 1, next_row, pltpu.roll(z, TOKEN_TILE - 1, 0))
    z_after = jnp.where(seq_pos == seq_len - 1, 0.0, z_after)
    conv = z_before * cw_ref[0:1, :] + z * cw_ref[1:2, :] + z_after * cw_ref[2:3, :] + cb_ref[...]
    conv_in = (bg_ref[...].astype(F32) * conv).astype(BF16)

    def merge(rs):
        attn_p = _dot(attn_ref[rs, :], wap_ref[...])
        conv_p = _dot(conv_in[rs, :], wcp_ref[...])
        return (jax.nn.sigmoid(ga_ref[rs, :].astype(F32)) * attn_p
                + jax.nn.sigmoid(gc_ref[rs, :].astype(F32)) * conv_p).astype(BF16)

    def mix_norm(merged, rs):
        mix = _dot(merged, wo_ref[...])
        x1 = _layer_norm(ALPHA * x_ref[rs, :] + g1 * mix, ln1g_ref[...], ln1b_ref[...])
        return x1, (x1 * (1.0 + sc2) + sh2).astype(BF16)

    def mlp_chunk(h2, c):
        cols = slice(c * FF_CHUNK, (c + 1) * FF_CHUNK)
        hidden = jnp.maximum(_dot(h2, w1_ref[:, cols]) + b1_ref[:, cols], 0.0)
        return _dot((hidden * hidden).astype(BF16), w2_ref[cols, :])

    def finish(x1, f, rs):
        o_ref[rs, :] = _layer_norm(ALPHA * x1 + g2 * f, ln2g_ref[...], ln2b_ref[...])

    groups = [slice(r0, r0 + POST_ROW_GROUP) for r0 in range(0, TOKEN_TILE, POST_ROW_GROUP)]
    n_chunks = D_FF // FF_CHUNK
    x1, h2 = mix_norm(merge(groups[0]), groups[0])
    f = b2_ref[...] + mlp_chunk(h2, 0)
    for n, rs in enumerate(groups):
        nxt = groups[n + 1] if n + 1 < len(groups) else None
        if nxt is not None:
            merged_next = merge(nxt)
        f = f + mlp_chunk(h2, 1)
        if nxt is not None:
            x1_next, h2_next = mix_norm(merged_next, nxt)
        for c in range(2, n_chunks):
            f = f + mlp_chunk(h2, c)
        if nxt is not None:
            f_next = b2_ref[...] + mlp_chunk(h2_next, 0)
        finish(x1, f, rs)
        if nxt is not None:
            x1, h2, f = x1_next, h2_next, f_next


def _post(x, attn, z, bg, ga, gc, mod, weights, layer, *, seq_len, mod_row0):
    n_tok = x.shape[0]
    assert seq_len & (seq_len - 1) == 0
    tps = max(seq_len // TOKEN_TILE, 1)
    n_tiles = n_tok // TOKEN_TILE
    halo_per_tile = TOKEN_TILE // HALO_ROWS
    n_halo_blocks = n_tok // HALO_ROWS
    tok = lambda width: pl.BlockSpec((TOKEN_TILE, width), lambda i: (i, 0))
    in_specs = [
        tok(D_MODEL), tok(ATTN_WIDTH), tok(CONV_WIDTH),
        pl.BlockSpec((HALO_ROWS, CONV_WIDTH), lambda i: (jnp.maximum(i * halo_per_tile - 1, 0), 0)),
        pl.BlockSpec((HALO_ROWS, CONV_WIDTH),
                     lambda i: (jnp.minimum((i + 1) * halo_per_tile, n_halo_blocks - 1), 0)),
        tok(CONV_WIDTH), tok(D_MODEL), tok(D_MODEL),
        pl.BlockSpec((1, COND_ROWS, 6 * D_MODEL), lambda i: (layer, 0, 0)),
    ]
    per_layer = weights["matrices"] + [weights["conv_w"]] + weights["vectors"]
    in_specs += [_layer_resident(w.shape[1:], layer) for w in per_layer]
    return pl.pallas_call(
        functools.partial(_post_kernel, mod_row0=mod_row0, tiles_per_seq=tps, seq_len=seq_len),
        grid=(n_tiles,),
        in_specs=in_specs,
        out_specs=tok(D_MODEL),
        out_shape=jax.ShapeDtypeStruct((n_tok, D_MODEL), F32),
        compiler_params=_params(1),
        name="post_ctx" if mod_row0 == 0 else "post_lat",
    )(x, attn, z, z, z, bg, ga, gc, mod, *per_layer)


def kernel(x_prompt, x_sample, cache_k, cache_v, c, c_ctx, w_mod, b_mod, w_in, rpb, conv_w, conv_b,
           w_attn_proj, w_conv_proj, w_o, ln1_g, ln1_b, w1, b1, w2, b2, ln2_g, ln2_b):
    batch, seq, _ = x_prompt.shape
    dec_batch, dec_seq, _ = x_sample.shape

    cond = jnp.concatenate(
        [c_ctx[None, :], c, jnp.zeros((COND_ROWS - 1 - dec_batch, D_MODEL), F32)], axis=0)
    mod = _modulation(cond, w_mod, b_mod)
    tables = _bias_tables(rpb)

    w_in_b = w_in.astype(BF16)
    weights = {
        "matrices": [w.astype(BF16) for w in (w_attn_proj, w_conv_proj, w_o, w1, w2)],
        "conv_w": conv_w,
        "vectors": [v.reshape(DEPTH, 1, -1) for v in (conv_b, ln1_g, ln1_b, b1, b2, ln2_g, ln2_b)],
    }

    xp = x_prompt.reshape(batch * seq, D_MODEL)
    xs = x_sample.reshape(dec_batch * dec_seq, D_MODEL)
    kv = None
    for layer in range(DEPTH):
        q, new_k, new_v, z, bg, ga, gc = _inproj(
            xp, mod, w_in_b, layer, n_seq=batch, seq_len=seq, mod_row0=0, kv_prev=kv)
        kv = (new_k, new_v)
        attn = _ctx_attention(q, new_k, new_v, layer)
        xp = _post(xp, attn, z, bg, ga, gc, mod, weights, layer, seq_len=seq, mod_row0=0)

        q, k, v, z, bg, ga, gc = _inproj(
            xs, mod, w_in_b, layer, n_seq=dec_batch, seq_len=dec_seq, mod_row0=1)
        attn = _lat_attention(q, k, v, cache_k, cache_v, tables, layer)
        xs = _post(xs, attn.reshape(dec_batch * dec_seq, ATTN_WIDTH), z, bg, ga, gc, mod, weights,
                   layer, seq_len=dec_seq, mod_row0=1)

    new_k, new_v = kv
    return (xp.reshape(batch, seq, D_MODEL), xs.reshape(dec_batch, dec_seq, D_MODEL), new_k, new_v)
```

```python
import functools

import jax
import jax.numpy as jnp
from jax import lax
from jax.experimental import pallas as pl
from jax.experimental.pallas import tpu as pltpu

D_MODEL = 1024
DEPTH = 4
N_HEADS = 8
HEAD_DIM = 64
ATTN_WIDTH = N_HEADS * HEAD_DIM
CONV_WIDTH = D_MODEL // 2
D_FF = 4 * D_MODEL
GRID_W = 64
WIN_R = 8
WIN_C = 16
RPB_ROWS = 2 * WIN_R - 1
RPB_COLS = 2 * WIN_C - 1
IN_COLS = 3 * ATTN_WIDTH + 3 * CONV_WIDTH + 2 * D_MODEL
SCALE = HEAD_DIM ** -0.5
ALPHA = (2.0 * DEPTH) ** 0.25
LN_EPS = 1e-5

COND_ROWS = 8
TOKEN_TILE = 512
FF_CHUNK = 1024
POST_ROW_GROUP = 256
HALO_ROWS = 16
ROWS_PER_CHUNK = 4
VMEM_LIMIT_BYTES = 56 * 1024 * 1024

BF16 = jnp.bfloat16
F32 = jnp.float32
NT_DIMS = (((1,), (1,)), ((), ()))


def _dot(a, b):
    return jnp.dot(a, b, preferred_element_type=F32)


def _dot_nt(a, b):
    return lax.dot_general(a, b, NT_DIMS, preferred_element_type=F32)


def _layer_resident(shape, layer):
    zeros = (0,) * len(shape)
    return pl.BlockSpec((None,) + tuple(shape), lambda *_: (layer,) + zeros,
                        pipeline_mode=pl.Buffered(1))


def _cond_row(tile, mod_row0, tiles_per_seq):
    if mod_row0 == 0:
        return 0
    return mod_row0 + tile // tiles_per_seq


def _params(n_grid_dims):
    return pltpu.CompilerParams(
        dimension_semantics=("arbitrary",) * n_grid_dims,
        vmem_limit_bytes=VMEM_LIMIT_BYTES,
    )


MOD_COL_TILE = 1536


def _mod_kernel(cond_ref, w_ref, b_ref, o_ref):
    c = cond_ref[...]
    s = (c * jax.nn.sigmoid(c)).astype(BF16)
    o_ref[0] = _dot(s, w_ref[0].astype(BF16)) + b_ref[0]


def _modulation(cond, w_mod, b_mod):
    n_cols = 6 * D_MODEL
    return pl.pallas_call(
        _mod_kernel,
        grid=(DEPTH, n_cols // MOD_COL_TILE),
        in_specs=[
            pl.BlockSpec((COND_ROWS, D_MODEL), lambda l, j: (0, 0)),
            pl.BlockSpec((1, D_MODEL, MOD_COL_TILE), lambda l, j: (l, 0, j)),
            pl.BlockSpec((1, 1, MOD_COL_TILE), lambda l, j: (l, 0, j)),
        ],
        out_specs=pl.BlockSpec((1, COND_ROWS, MOD_COL_TILE), lambda l, j: (l, 0, j)),
        out_shape=jax.ShapeDtypeStruct((DEPTH, COND_ROWS, n_cols), F32),
        compiler_params=_params(2),
        name="modulation",
    )(cond, w_mod, b_mod.reshape(DEPTH, 1, n_cols))


UNION_ROWS = WIN_R + ROWS_PER_CHUNK
N_CHUNK_KINDS = 3
assert ROWS_PER_CHUNK == WIN_R // 2


def _bias_table_kernel(rpb_ref, o_ref):
    l = pl.program_id(0)
    h = pl.program_id(1)
    base = (l * N_HEADS + h) * (RPB_ROWS * RPB_COLS)
    shape = (GRID_W, 2 * GRID_W)
    q = lax.broadcasted_iota(jnp.int32, shape, 0)
    lane = lax.broadcasted_iota(jnp.int32, shape, 1)
    kc = lane & (GRID_W - 1)
    t = kc - q + (WIN_C - 1)
    col_start = jnp.clip(q - WIN_C // 2, 0, GRID_W - WIN_C)
    valid = (kc >= col_start) & (kc < col_start + WIN_C)
    neg_inf = jnp.full(shape, -jnp.inf, F32)
    per_row = []
    for j in range(RPB_ROWS):
        acc = neg_inf
        for m in range(RPB_COLS):
            acc = jnp.where(t == m, rpb_ref[base + j * RPB_COLS + m], acc)
        per_row.append(jnp.where(valid, acc, neg_inf))
    chunk_kinds = (
        [(0, g) for g in range(ROWS_PER_CHUNK)],
        [(g, WIN_R // 2) for g in range(ROWS_PER_CHUNK)],
        [(ROWS_PER_CHUNK, WIN_R // 2 + g) for g in range(ROWS_PER_CHUNK)],
    )
    for kind, rows in enumerate(chunk_kinds):
        for g, (offset, d) in enumerate(rows):
            for pair in range(UNION_ROWS // 2):
                halves = []
                for j in (2 * pair, 2 * pair + 1):
                    i = j - offset
                    halves.append(per_row[i + WIN_R - 1 - d] if 0 <= i < WIN_R else neg_inf)
                o_ref[0, 0, kind, g * GRID_W:(g + 1) * GRID_W, pair * 128:(pair + 1) * 128] = (
                    jnp.where(lane < GRID_W, halves[0], halves[1]))


def _bias_tables(rpb):
    blk = (1, 1, N_CHUNK_KINDS, ROWS_PER_CHUNK * GRID_W, UNION_ROWS * GRID_W)
    return pl.pallas_call(
        _bias_table_kernel,
        grid=(DEPTH, N_HEADS),
        in_specs=[pl.BlockSpec(memory_space=pltpu.SMEM)],
        out_specs=pl.BlockSpec(blk, lambda l, h: (l, h, 0, 0, 0)),
        out_shape=jax.ShapeDtypeStruct((DEPTH, N_HEADS) + blk[2:], F32),
        compiler_params=_params(2),
        name="bias_tables",
    )(rpb.reshape(-1))


def _inproj_kernel(x_ref, mod_ref, w_ref, *rest, mod_row0, tiles_per_seq, seqs_per_tile):
    q_ref, k_ref, v_ref, z_ref, bg_ref, ga_ref, gc_ref = rest[-7:]
    i = pl.program_id(0)
    row = _cond_row(i, mod_row0, tiles_per_seq)
    sh1 = mod_ref[0, pl.ds(row, 1), 0:D_MODEL]
    sc1 = mod_ref[0, pl.ds(row, 1), D_MODEL:2 * D_MODEL]
    h = (x_ref[...] * (1.0 + sc1) + sh1).astype(BF16)

    def proj(lo, hi):
        return _dot(h, w_ref[:, lo:hi])

    a = ATTN_WIDTH
    c0 = 3 * ATTN_WIDTH
    g0 = c0 + 3 * CONV_WIDTH
    ga_ref[...] = proj(g0, g0 + D_MODEL).astype(BF16)
    gc_ref[...] = proj(g0 + D_MODEL, g0 + 2 * D_MODEL).astype(BF16)
    rows = TOKEN_TILE // seqs_per_tile
    for lo, ref, scale in ((0, q_ref, SCALE), (a, k_ref, None), (2 * a, v_ref, None)):
        y = proj(lo, lo + a)
        if scale is not None:
            y = y * scale
        for s in range(seqs_per_tile):
            for hd in range(N_HEADS):
                piece = y[s * rows:(s + 1) * rows, hd * HEAD_DIM:(hd + 1) * HEAD_DIM]
                if ref is q_ref:
                    ref[s, hd] = piece.astype(ref.dtype)
                else:
                    ref[s, 0, hd] = piece.astype(ref.dtype)
    u = proj(c0, c0 + CONV_WIDTH)
    cg = proj(c0 + 2 * CONV_WIDTH, c0 + 3 * CONV_WIDTH)
    z_ref[...] = (cg * u).astype(BF16)
    bg_ref[...] = proj(c0 + CONV_WIDTH, c0 + 2 * CONV_WIDTH).astype(BF16)


def _inproj(x, mod, w_in, layer, *, n_seq, seq_len, mod_row0, kv_prev=None):
    n_tok = n_seq * seq_len
    n_tiles = n_tok // TOKEN_TILE
    is_ctx = mod_row0 == 0
    if seq_len >= TOKEN_TILE:
        tps, spt, rows = seq_len // TOKEN_TILE, 1, TOKEN_TILE
        head_map = lambda i: (i // tps, 0, i % tps, 0)
        kv_map = lambda i: (i // tps, layer if is_ctx else 0, 0, i % tps, 0)
    else:
        tps, spt, rows = 1, TOKEN_TILE // seq_len, seq_len
        head_map = lambda i: (i, 0, 0, 0)
        kv_map = lambda i: (i, layer if is_ctx else 0, 0, 0, 0)
    head_blk = (spt, N_HEADS, rows, HEAD_DIM)
    kv_blk = (spt, 1, N_HEADS, rows, HEAD_DIM)
    if is_ctx:
        kv_shape = jax.ShapeDtypeStruct((n_seq, DEPTH, N_HEADS, seq_len, HEAD_DIM), F32)
    else:
        kv_shape = jax.ShapeDtypeStruct((n_seq, 1, N_HEADS, seq_len, HEAD_DIM), BF16)
    tok = lambda width: pl.BlockSpec((TOKEN_TILE, width), lambda i: (i, 0))
    tok_shape = lambda width: jax.ShapeDtypeStruct((n_tok, width), BF16)
    in_specs = [
        tok(D_MODEL),
        pl.BlockSpec((1, COND_ROWS, 6 * D_MODEL), lambda i: (layer, 0, 0)),
        _layer_resident((D_MODEL, IN_COLS), layer),
    ]
    args = [x, mod, w_in]
    aliases = {}
    if kv_prev is not None:
        in_specs += [pl.BlockSpec(memory_space=pl.ANY)] * 2
        args += list(kv_prev)
        aliases = {3: 1, 4: 2}
    return pl.pallas_call(
        functools.partial(_inproj_kernel, mod_row0=mod_row0, tiles_per_seq=tps, seqs_per_tile=spt),
        grid=(n_tiles,),
        in_specs=in_specs,
        out_specs=[
            pl.BlockSpec(head_blk, head_map),
            pl.BlockSpec(kv_blk, kv_map),
            pl.BlockSpec(kv_blk, kv_map),
            tok(CONV_WIDTH), tok(CONV_WIDTH), tok(D_MODEL), tok(D_MODEL),
        ],
        out_shape=[
            jax.ShapeDtypeStruct((n_seq, N_HEADS, seq_len, HEAD_DIM), BF16),
            kv_shape, kv_shape,
            tok_shape(CONV_WIDTH), tok_shape(CONV_WIDTH), tok_shape(D_MODEL), tok_shape(D_MODEL),
        ],
        input_output_aliases=aliases,
        compiler_params=_params(1),
        name="inproj_ctx" if is_ctx else "inproj_lat",
    )(*args)


def _software_pipeline(n_items, score, normalise, finish):
    scores = {0: score(0)}
    probs = {}
    for i in range(n_items + 1):
        if i + 1 < n_items:
            scores[i + 1] = score(i + 1)
        if i < n_items:
            probs[i] = normalise(i, scores.pop(i))
        if i >= 1:
            finish(i - 1, probs.pop(i - 1))


CTX_SEQS_PER_STEP = 4


def _ctx_attn_kernel(q_ref, k_ref, v_ref, o_ref):
    seq_len = q_ref.shape[2]
    items = [(s, hd) for s in range(CTX_SEQS_PER_STEP) for hd in range(N_HEADS)]
    done = {}

    def score(i):
        s, hd = items[i]
        return _dot_nt(q_ref[s, hd], k_ref[s, 0, hd].astype(BF16))

    def normalise(i, sc):
        p = jnp.exp(sc - jnp.max(sc, axis=-1, keepdims=True))
        return p.astype(BF16), jnp.sum(p, axis=-1, keepdims=True)

    def finish(i, prob):
        s, hd = items[i]
        p, denom = prob
        done[hd] = _dot(p, v_ref[s, 0, hd].astype(BF16)) / denom
        if hd % 2 == 1:
            o_ref[s * seq_len:(s + 1) * seq_len, (hd - 1) * HEAD_DIM:(hd + 1) * HEAD_DIM] = (
                jnp.concatenate([done.pop(hd - 1), done.pop(hd)], axis=-1).astype(BF16))

    _software_pipeline(len(items), score, normalise, finish)


def _ctx_attention(q, new_k, new_v, layer):
    n_seq, _, seq_len, _ = q.shape
    sps = CTX_SEQS_PER_STEP
    kv_blk = (sps, 1, N_HEADS, seq_len, HEAD_DIM)
    kv_map = lambda b: (b, layer, 0, 0, 0)
    return pl.pallas_call(
        _ctx_attn_kernel,
        grid=(n_seq // sps,),
        in_specs=[
            pl.BlockSpec((sps, N_HEADS, seq_len, HEAD_DIM), lambda b: (b, 0, 0, 0)),
            pl.BlockSpec(kv_blk, kv_map),
            pl.BlockSpec(kv_blk, kv_map),
        ],
        out_specs=pl.BlockSpec((sps * seq_len, ATTN_WIDTH), lambda b: (b, 0)),
        out_shape=jax.ShapeDtypeStruct((n_seq * seq_len, ATTN_WIDTH), BF16),
        compiler_params=_params(1),
        name="attn_ctx",
    )(q, new_k, new_v)


HEADS_PER_STEP = 2


def _lat_attn_kernel(q_ref, k_ref, v_ref, kc_ref, vc_ref, tab_ref, o_ref, *, n_rows):
    n_chunks = n_rows // ROWS_PER_CHUNK
    chunk_q = ROWS_PER_CHUNK * GRID_W
    union = UNION_ROWS * GRID_W
    items = [(c, hh) for c in range(n_chunks) for hh in range(HEADS_PER_STEP)]
    done = {}

    def key_rows(c):
        first_key_row = min(max(c * ROWS_PER_CHUNK - WIN_R // 2, 0), n_rows - UNION_ROWS)
        return slice(first_key_row * GRID_W, first_key_row * GRID_W + union)

    def score(i):
        c, hh = items[i]
        kind = 0 if c == 0 else (2 if c == n_chunks - 1 else 1)
        qc = q_ref[0, hh, c * chunk_q:(c + 1) * chunk_q, :]
        s_band = _dot_nt(qc, k_ref[0, 0, hh, key_rows(c), :]) + tab_ref[0, hh, kind]
        s_ctx = _dot_nt(qc, kc_ref[0, 0, hh].astype(BF16))
        return s_band, s_ctx

    def normalise(i, scores):
        s_band, s_ctx = scores
        m = jnp.maximum(jnp.max(s_band, axis=-1, keepdims=True),
                        jnp.max(s_ctx, axis=-1, keepdims=True))
        p_band = jnp.exp(s_band - m)
        p_ctx = jnp.exp(s_ctx - m)
        denom = jnp.sum(p_band, axis=-1, keepdims=True) + jnp.sum(p_ctx, axis=-1, keepdims=True)
        return p_band.astype(BF16), p_ctx.astype(BF16), denom

    def finish(i, prob):
        c, hh = items[i]
        p_band, p_ctx, denom = prob
        o = _dot(p_band, v_ref[0, 0, hh, key_rows(c), :]) + _dot(p_ctx, vc_ref[0, 0, hh].astype(BF16))
        done[hh] = o / denom
        if hh == HEADS_PER_STEP - 1:
            o_ref[0, c * chunk_q:(c + 1) * chunk_q, :] = jnp.concatenate(
                [done.pop(h) for h in range(HEADS_PER_STEP)], axis=-1).astype(BF16)

    _software_pipeline(len(items), score, normalise, finish)


def _lat_attention(q, k, v, cache_k, cache_v, tables, layer):
    n_seq, _, seq_len, _ = q.shape
    past_len = cache_k.shape[3]
    n_rows = seq_len // GRID_W
    assert n_rows % ROWS_PER_CHUNK == 0 and n_rows >= 2 * UNION_ROWS
    hps = HEADS_PER_STEP
    return pl.pallas_call(
        functools.partial(_lat_attn_kernel, n_rows=n_rows),
        grid=(N_HEADS // hps, n_seq),
        in_specs=[
            pl.BlockSpec((1, hps, seq_len, HEAD_DIM), lambda p, b: (b, p, 0, 0)),
            pl.BlockSpec((1, 1, hps, seq_len, HEAD_DIM), lambda p, b: (b, 0, p, 0, 0)),
            pl.BlockSpec((1, 1, hps, seq_len, HEAD_DIM), lambda p, b: (b, 0, p, 0, 0)),
            pl.BlockSpec((1, 1, hps, past_len, HEAD_DIM), lambda p, b: (b, layer, p, 0, 0)),
            pl.BlockSpec((1, 1, hps, past_len, HEAD_DIM), lambda p, b: (b, layer, p, 0, 0)),
            pl.BlockSpec((1, hps) + tables.shape[2:], lambda p, b: (layer, p, 0, 0, 0)),
        ],
        out_specs=pl.BlockSpec((1, seq_len, hps * HEAD_DIM), lambda p, b: (b, 0, p)),
        out_shape=jax.ShapeDtypeStruct((n_seq, seq_len, ATTN_WIDTH), BF16),
        compiler_params=_params(2),
        name="attn_lat",
    )(q, k, v, cache_k, cache_v, tables)


def _layer_norm(y, g, b):
    mu = jnp.mean(y, axis=-1, keepdims=True)
    yc = y - mu
    var = jnp.mean(yc * yc, axis=-1, keepdims=True)
    return yc * lax.rsqrt(var + LN_EPS) * g + b


def _post_kernel(x_ref, attn_ref, z_ref, zprev_ref, znext_ref, bg_ref, ga_ref, gc_ref, mod_ref,
                 wap_ref, wcp_ref, wo_ref, w1_ref, w2_ref, cw_ref, cb_ref,
                 ln1g_ref, ln1b_ref, b1_ref, b2_ref, ln2g_ref, ln2b_ref, o_ref,
                 *, mod_row0, tiles_per_seq, seq_len):
    i = pl.program_id(0)
    row = _cond_row(i, mod_row0, tiles_per_seq)
    d = D_MODEL
    g1 = mod_ref[0, pl.ds(row, 1), 2 * d:3 * d]
    sh2 = mod_ref[0, pl.ds(row, 1), 3 * d:4 * d]
    sc2 = mod_ref[0, pl.ds(row, 1), 4 * d:5 * d]
    g2 = mod_ref[0, pl.ds(row, 1), 5 * d:6 * d]

    z = z_ref[...].astype(F32)
    prev_row = zprev_ref[...].astype(F32)[HALO_ROWS - 1:HALO_ROWS, :]
    next_row = znext_ref[...].astype(F32)[0:1, :]
    pos = lax.broadcasted_iota(jnp.int32, z.shape, 0)
    seq_pos = (i * TOKEN_TILE + pos) & (seq_len - 1)
    z_before = jnp.where(pos == 0, prev_row, pltpu.roll(z, 1, 0))
    z_before = jnp.where(seq_pos == 0, 0.0, z_before)
    z_after = jnp.where(pos == TOKEN_TILE - 1, next_row, pltpu.roll(z, TOKEN_TILE - 1, 0))
    z_after = jnp.where(seq_pos == seq_len - 1, 0.0, z_after)
    conv = z_before * cw_ref[0:1, :] + z * cw_ref[1:2, :] + z_after * cw_ref[2:3, :] + cb_ref[...]
    conv_in = (bg_ref[...].astype(F32) * conv).astype(BF16)

    def merge(rs):
        attn_p = _dot(attn_ref[rs, :], wap_ref[...])
        conv_p = _dot(conv_in[rs, :], wcp_ref[...])
        return (jax.nn.sigmoid(ga_ref[rs, :].astype(F32)) * attn_p
                + jax.nn.sigmoid(gc_ref[rs, :].astype(F32)) * conv_p).astype(BF16)

    def mix_norm(merged, rs):
        mix = _dot(merged, wo_ref[...])
        x1 = _layer_norm(ALPHA * x_ref[rs, :] + g1 * mix, ln1g_ref[...], ln1b_ref[...])
        return x1, (x1 * (1.0 + sc2) + sh2).astype(BF16)

    def mlp_chunk(h2, c):
        cols = slice(c * FF_CHUNK, (c + 1) * FF_CHUNK)
        hidden = jnp.maximum(_dot(h2, w1_ref[:, cols]) + b1_ref[:, cols], 0.0)
        return _dot((hidden * hidden).astype(BF16), w2_ref[cols, :])

    def finish(x1, f, rs):
        o_ref[rs, :] = _layer_norm(ALPHA * x1 + g2 * f, ln2g_ref[...], ln2b_ref[...])

    groups = [slice(r0, r0 + POST_ROW_GROUP) for r0 in range(0, TOKEN_TILE, POST_ROW_GROUP)]
    n_chunks = D_FF // FF_CHUNK
    x1, h2 = mix_norm(merge(groups[0]), groups[0])
    f = b2_ref[...] + mlp_chunk(h2, 0)
    for n, rs in enumerate(groups):
        nxt = groups[n + 1] if n + 1 < len(groups) else None
        if nxt is not None:
            merged_next = merge(nxt)
        f = f + mlp_chunk(h2, 1)
        if nxt is not None:
            x1_next, h2_next = mix_norm(merged_next, nxt)
        for c in range(2, n_chunks):
            f = f + mlp_chunk(h2, c)
        if nxt is not None:
            f_next = b2_ref[...] + mlp_chunk(h2_next, 0)
        finish(x1, f, rs)
        if nxt is not None:
            x1, h2, f = x1_next, h2_next, f_next


def _post(x, attn, z, bg, ga, gc, mod, weights, layer, *, seq_len, mod_row0):
    n_tok = x.shape[0]
    assert seq_len & (seq_len - 1) == 0
    tps = max(seq_len // TOKEN_TILE, 1)
    n_tiles = n_tok // TOKEN_TILE
    halo_per_tile = TOKEN_TILE // HALO_ROWS
    n_halo_blocks = n_tok // HALO_ROWS
    tok = lambda width: pl.BlockSpec((TOKEN_TILE, width), lambda i: (i, 0))
    in_specs = [
        tok(D_MODEL), tok(ATTN_WIDTH), tok(CONV_WIDTH),
        pl.BlockSpec((HALO_ROWS, CONV_WIDTH), lambda i: (jnp.maximum(i * halo_per_tile - 1, 0), 0)),
        pl.BlockSpec((HALO_ROWS, CONV_WIDTH),
                     lambda i: (jnp.minimum((i + 1) * halo_per_tile, n_halo_blocks - 1), 0)),
        tok(CONV_WIDTH), tok(D_MODEL), tok(D_MODEL),
        pl.BlockSpec((1, COND_ROWS, 6 * D_MODEL), lambda i: (layer, 0, 0)),
    ]
    per_layer = weights["matrices"] + [weights["conv_w"]] + weights["vectors"]
    in_specs += [_layer_resident(w.shape[1:], layer) for w in per_layer]
    return pl.pallas_call(
        functools.partial(_post_kernel, mod_row0=mod_row0, tiles_per_seq=tps, seq_len=seq_len),
        grid=(n_tiles,),
        in_specs=in_specs,
        out_specs=tok(D_MODEL),
        out_shape=jax.ShapeDtypeStruct((n_tok, D_MODEL), F32),
        compiler_params=_params(1),
        name="post_ctx" if mod_row0 == 0 else "post_lat",
    )(x, attn, z, z, z, bg, ga, gc, mod, *per_layer)


def kernel(x_prompt, x_sample, cache_k, cache_v, c, c_ctx, w_mod, b_mod, w_in, rpb, conv_w, conv_b,
           w_attn_proj, w_conv_proj, w_o, ln1_g, ln1_b, w1, b1, w2, b2, ln2_g, ln2_b):
    batch, seq, _ = x_prompt.shape
    dec_batch, dec_seq, _ = x_sample.shape

    cond = jnp.concatenate(
        [c_ctx[None, :], c, jnp.zeros((COND_ROWS - 1 - dec_batch, D_MODEL), F32)], axis=0)
    mod = _modulation(cond, w_mod, b_mod)
    tables = _bias_tables(rpb)

    w_in_b = w_in.astype(BF16)
    weights = {
        "matrices": [w.astype(BF16) for w in (w_attn_proj, w_conv_proj, w_o, w1, w2)],
        "conv_w": conv_w,
        "vectors": [v.reshape(DEPTH, 1, -1) for v in (conv_b, ln1_g, ln1_b, b1, b2, ln2_g, ln2_b)],
    }

    xp = x_prompt.reshape(batch * seq, D_MODEL)
    xs = x_sample.reshape(dec_batch * dec_seq, D_MODEL)
    kv = None
    for layer in range(DEPTH):
        q, new_k, new_v, z, bg, ga, gc = _inproj(
            xp, mod, w_in_b, layer, n_seq=batch, seq_len=seq, mod_row0=0, kv_prev=kv)
        kv = (new_k, new_v)
        attn = _ctx_attention(q, new_k, new_v, layer)
        xp = _post(xp, attn, z, bg, ga, gc, mod, weights, layer, seq_len=seq, mod_row0=0)

        q, k, v, z, bg, ga, gc = _inproj(
            xs, mod, w_in_b, layer, n_seq=dec_batch, seq_len=dec_seq, mod_row0=1)
        attn = _lat_attention(q, k, v, cache_k, cache_v, tables, layer)
        xs = _post(xs, attn.reshape(dec_batch * dec_seq, ATTN_WIDTH), z, bg, ga, gc, mod, weights,
                   layer, seq_len=dec_seq, mod_row0=1)

    new_k, new_v = kv
    return (xp.reshape(batch, seq, D_MODEL), xs.reshape(dec_batch, dec_seq, D_MODEL), new_k, new_v)
```

```python
import functools

import jax
import jax.numpy as jnp
from jax import lax
from jax.experimental import pallas as pl
from jax.experimental.pallas import tpu as pltpu

D_MODEL = 1024
DEPTH = 4
N_HEADS = 8
HEAD_DIM = 64
ATTN_WIDTH = N_HEADS * HEAD_DIM
CONV_WIDTH = D_MODEL // 2
D_FF = 4 * D_MODEL
GRID_W = 64
WIN_R = 8
WIN_C = 16
RPB_ROWS = 2 * WIN_R - 1
RPB_COLS = 2 * WIN_C - 1
IN_COLS = 3 * ATTN_WIDTH + 3 * CONV_WIDTH + 2 * D_MODEL
SCALE = HEAD_DIM ** -0.5
ALPHA = (2.0 * DEPTH) ** 0.25
LN_EPS = 1e-5

COND_ROWS = 8
TOKEN_TILE = 512
FF_CHUNK = 1024
POST_ROW_GROUP = 256
HALO_ROWS = 16
ROWS_PER_CHUNK = 4
VMEM_LIMIT_BYTES = 56 * 1024 * 1024

BF16 = jnp.bfloat16
F32 = jnp.float32
NT_DIMS = (((1,), (1,)), ((), ()))


def _dot(a, b):
    return jnp.dot(a, b, preferred_element_type=F32)


def _dot_nt(a, b):
    return lax.dot_general(a, b, NT_DIMS, preferred_element_type=F32)


def _layer_resident(shape, layer):
    zeros = (0,) * len(shape)
    return pl.BlockSpec((None,) + tuple(shape), lambda *_: (layer,) + zeros,
                        pipeline_mode=pl.Buffered(1))


def _cond_row(tile, mod_row0, tiles_per_seq):
    if mod_row0 == 0:
        return 0
    return mod_row0 + tile // tiles_per_seq


def _params(n_grid_dims):
    return pltpu.CompilerParams(
        dimension_semantics=("arbitrary",) * n_grid_dims,
        vmem_limit_bytes=VMEM_LIMIT_BYTES,
    )


MOD_COL_TILE = 1536


def _mod_kernel(cond_ref, w_ref, b_ref, o_ref):
    c = cond_ref[...]
    s = (c * jax.nn.sigmoid(c)).astype(BF16)
    o_ref[0] = _dot(s, w_ref[0].astype(BF16)) + b_ref[0]


def _modulation(cond, w_mod, b_mod):
    n_cols = 6 * D_MODEL
    return pl.pallas_call(
        _mod_kernel,
        grid=(DEPTH, n_cols // MOD_COL_TILE),
        in_specs=[
            pl.BlockSpec((COND_ROWS, D_MODEL), lambda l, j: (0, 0)),
            pl.BlockSpec((1, D_MODEL, MOD_COL_TILE), lambda l, j: (l, 0, j)),
            pl.BlockSpec((1, 1, MOD_COL_TILE), lambda l, j: (l, 0, j)),
        ],
        out_specs=pl.BlockSpec((1, COND_ROWS, MOD_COL_TILE), lambda l, j: (l, 0, j)),
        out_shape=jax.ShapeDtypeStruct((DEPTH, COND_ROWS, n_cols), F32),
        compiler_params=_params(2),
        name="modulation",
    )(cond, w_mod, b_mod.reshape(DEPTH, 1, n_cols))


UNION_ROWS = WIN_R + ROWS_PER_CHUNK
N_CHUNK_KINDS = 3
assert ROWS_PER_CHUNK == WIN_R // 2


def _bias_table_kernel(rpb_ref, o_ref):
    l = pl.program_id(0)
    h = pl.program_id(1)
    base = (l * N_HEADS + h) * (RPB_ROWS * RPB_COLS)
    shape = (GRID_W, 2 * GRID_W)
    q = lax.broadcasted_iota(jnp.int32, shape, 0)
    lane = lax.broadcasted_iota(jnp.int32, shape, 1)
    kc = lane & (GRID_W - 1)
    t = kc - q + (WIN_C - 1)
    col_start = jnp.clip(q - WIN_C // 2, 0, GRID_W - WIN_C)
    valid = (kc >= col_start) & (kc < col_start + WIN_C)
    neg_inf = jnp.full(shape, -jnp.inf, F32)
    per_row = []
    for j in range(RPB_ROWS):
        acc = neg_inf
        for m in range(RPB_COLS):
            acc = jnp.where(t == m, rpb_ref[base + j * RPB_COLS + m], acc)
        per_row.append(jnp.where(valid, acc, neg_inf))
    chunk_kinds = (
        [(0, g) for g in range(ROWS_PER_CHUNK)],
        [(g, WIN_R // 2) for g in range(ROWS_PER_CHUNK)],
        [(ROWS_PER_CHUNK, WIN_R // 2 + g) for g in range(ROWS_PER_CHUNK)],
    )
    for kind, rows in enumerate(chunk_kinds):
        for g, (offset, d) in enumerate(rows):
            for pair in range(UNION_ROWS // 2):
                halves = []
                for j in (2 * pair, 2 * pair + 1):
                    i = j - offset
                    halves.append(per_row[i + WIN_R - 1 - d] if 0 <= i < WIN_R else neg_inf)
                o_ref[0, 0, kind, g * GRID_W:(g + 1) * GRID_W, pair * 128:(pair + 1) * 128] = (
                    jnp.where(lane < GRID_W, halves[0], halves[1]))


def _bias_tables(rpb):
    blk = (1, 1, N_CHUNK_KINDS, ROWS_PER_CHUNK * GRID_W, UNION_ROWS * GRID_W)
    return pl.pallas_call(
        _bias_table_kernel,
        grid=(DEPTH, N_HEADS),
        in_specs=[pl.BlockSpec(memory_space=pltpu.SMEM)],
        out_specs=pl.BlockSpec(blk, lambda l, h: (l, h, 0, 0, 0)),
        out_shape=jax.ShapeDtypeStruct((DEPTH, N_HEADS) + blk[2:], F32),
        compiler_params=_params(2),
        name="bias_tables",
    )(rpb.reshape(-1))


def _transpose_cast_kernel(w_ref, o_ref):
    o_ref[...] = w_ref[...].T.astype(BF16)


def _transposed_kv_weights(w_in):
    return pl.pallas_call(
        _transpose_cast_kernel,
        grid=(DEPTH, 2),
        in_specs=[pl.BlockSpec((None, D_MODEL, ATTN_WIDTH), lambda l, j: (l, 0, 1 + j))],
        out_specs=pl.BlockSpec((None, ATTN_WIDTH, D_MODEL), lambda l, j: (l, j, 0)),
        out_shape=jax.ShapeDtypeStruct((DEPTH, 2 * ATTN_WIDTH, D_MODEL), BF16),
        compiler_params=_params(2),
        name="kv_weight_transpose",
    )(w_in)


def _inproj_kernel(x_ref, mod_ref, w_ref, wkv_t_ref, *rest, mod_row0, tiles_per_seq, seqs_per_tile):
    q_ref, kt_ref, vt_ref, z_ref, bg_ref, ga_ref, gc_ref = rest[-7:]
    i = pl.program_id(0)
    row = _cond_row(i, mod_row0, tiles_per_seq)
    sh1 = mod_ref[0, pl.ds(row, 1), 0:D_MODEL]
    sc1 = mod_ref[0, pl.ds(row, 1), D_MODEL:2 * D_MODEL]
    h = (x_ref[...] * (1.0 + sc1) + sh1).astype(BF16)

    def proj(lo, hi):
        return _dot(h, w_ref[:, lo:hi])

    a = ATTN_WIDTH
    c0 = 3 * ATTN_WIDTH
    g0 = c0 + 3 * CONV_WIDTH
    ga_ref[...] = proj(g0, g0 + D_MODEL).astype(BF16)
    gc_ref[...] = proj(g0 + D_MODEL, g0 + 2 * D_MODEL).astype(BF16)
    rows = TOKEN_TILE // seqs_per_tile
    q = proj(0, a) * SCALE
    for s in range(seqs_per_tile):
        for hd in range(N_HEADS):
            q_ref[s, hd] = q[s * rows:(s + 1) * rows, hd * HEAD_DIM:(hd + 1) * HEAD_DIM].astype(BF16)
    for lo, ref in ((0, kt_ref), (a, vt_ref)):
        y_t = _dot_nt(wkv_t_ref[lo:lo + a, :], h)
        for s in range(seqs_per_tile):
            for hd in range(N_HEADS):
                ref[s, 0, hd] = y_t[hd * HEAD_DIM:(hd + 1) * HEAD_DIM,
                                    s * rows:(s + 1) * rows].astype(ref.dtype)
    u = proj(c0, c0 + CONV_WIDTH)
    cg = proj(c0 + 2 * CONV_WIDTH, c0 + 3 * CONV_WIDTH)
    z_ref[...] = (cg * u).astype(BF16)
    bg_ref[...] = proj(c0 + CONV_WIDTH, c0 + 2 * CONV_WIDTH).astype(BF16)


def _inproj(x, mod, w_in, w_kv_t, layer, *, n_seq, seq_len, mod_row0, kv_prev=None):
    n_tok = n_seq * seq_len
    n_tiles = n_tok // TOKEN_TILE
    is_ctx = mod_row0 == 0
    if seq_len >= TOKEN_TILE:
        tps, spt, rows = seq_len // TOKEN_TILE, 1, TOKEN_TILE
        head_map = lambda i: (i // tps, 0, i % tps, 0)
        kv_map = lambda i: (i // tps, layer if is_ctx else 0, 0, 0, i % tps)
    else:
        tps, spt, rows = 1, TOKEN_TILE // seq_len, seq_len
        head_map = lambda i: (i, 0, 0, 0)
        kv_map = lambda i: (i, layer if is_ctx else 0, 0, 0, 0)
    head_blk = (spt, N_HEADS, rows, HEAD_DIM)
    kv_blk = (spt, 1, N_HEADS, HEAD_DIM, rows)
    if is_ctx:
        kv_shape = jax.ShapeDtypeStruct((n_seq, DEPTH, N_HEADS, HEAD_DIM, seq_len), F32)
    else:
        kv_shape = jax.ShapeDtypeStruct((n_seq, 1, N_HEADS, HEAD_DIM, seq_len), BF16)
    tok = lambda width: pl.BlockSpec((TOKEN_TILE, width), lambda i: (i, 0))
    tok_shape = lambda width: jax.ShapeDtypeStruct((n_tok, width), BF16)
    in_specs = [
        tok(D_MODEL),
        pl.BlockSpec((1, COND_ROWS, 6 * D_MODEL), lambda i: (layer, 0, 0)),
        _layer_resident((D_MODEL, IN_COLS), layer),
        _layer_resident((2 * ATTN_WIDTH, D_MODEL), layer),
    ]
    args = [x, mod, w_in, w_kv_t]
    aliases = {}
    if kv_prev is not None:
        in_specs += [pl.BlockSpec(memory_space=pl.ANY)] * 2
        args += list(kv_prev)
        aliases = {4: 1, 5: 2}
    return pl.pallas_call(
        functools.partial(_inproj_kernel, mod_row0=mod_row0, tiles_per_seq=tps, seqs_per_tile=spt),
        grid=(n_tiles,),
        in_specs=in_specs,
        out_specs=[
            pl.BlockSpec(head_blk, head_map),
            pl.BlockSpec(kv_blk, kv_map),
            pl.BlockSpec(kv_blk, kv_map),
            tok(CONV_WIDTH), tok(CONV_WIDTH), tok(D_MODEL), tok(D_MODEL),
        ],
        out_shape=[
            jax.ShapeDtypeStruct((n_seq, N_HEADS, seq_len, HEAD_DIM), BF16),
            kv_shape, kv_shape,
            tok_shape(CONV_WIDTH), tok_shape(CONV_WIDTH), tok_shape(D_MODEL), tok_shape(D_MODEL),
        ],
        input_output_aliases=aliases,
        compiler_params=_params(1),
        name="inproj_ctx" if is_ctx else "inproj_lat",
    )(*args)


def _software_pipeline(n_items, score, normalise, finish):
    scores = {0: score(0)}
    probs = {}
    for i in range(n_items + 1):
        if i + 1 < n_items:
            scores[i + 1] = score(i + 1)
        if i < n_items:
            probs[i] = normalise(i, scores.pop(i))
        if i >= 1:
            finish(i - 1, probs.pop(i - 1))


CTX_SEQS_PER_STEP = 4


def _ctx_attn_kernel(q_ref, kt_ref, vt_ref, o_ref):
    seq_len = q_ref.shape[2]
    items = [(s, hd) for s in range(CTX_SEQS_PER_STEP) for hd in range(N_HEADS)]
    done = {}

    def score(i):
        s, hd = items[i]
        return _dot(q_ref[s, hd], kt_ref[s, 0, hd].astype(BF16))

    def normalise(i, sc):
        p = jnp.exp(sc - jnp.max(sc, axis=-1, keepdims=True))
        return p.astype(BF16), jnp.sum(p, axis=-1, keepdims=True)

    def finish(i, prob):
        s, hd = items[i]
        p, denom = prob
        done[hd] = _dot_nt(p, vt_ref[s, 0, hd].astype(BF16)) / denom
        if hd % 2 == 1:
            o_ref[s * seq_len:(s + 1) * seq_len, (hd - 1) * HEAD_DIM:(hd + 1) * HEAD_DIM] = (
                jnp.concatenate([done.pop(hd - 1), done.pop(hd)], axis=-1).astype(BF16))

    _software_pipeline(len(items), score, normalise, finish)


def _ctx_attention(q, new_k, new_v, layer):
    n_seq, _, seq_len, _ = q.shape
    sps = CTX_SEQS_PER_STEP
    kv_blk = (sps, 1, N_HEADS, HEAD_DIM, seq_len)
    kv_map = lambda b: (b, layer, 0, 0, 0)
    return pl.pallas_call(
        _ctx_attn_kernel,
        grid=(n_seq // sps,),
        in_specs=[
            pl.BlockSpec((sps, N_HEADS, seq_len, HEAD_DIM), lambda b: (b, 0, 0, 0)),
            pl.BlockSpec(kv_blk, kv_map),
            pl.BlockSpec(kv_blk, kv_map),
        ],
        out_specs=pl.BlockSpec((sps * seq_len, ATTN_WIDTH), lambda b: (b, 0)),
        out_shape=jax.ShapeDtypeStruct((n_seq * seq_len, ATTN_WIDTH), BF16),
        compiler_params=_params(1),
        name="attn_ctx",
    )(q, new_k, new_v)


HEADS_PER_STEP = 2


def _lat_attn_kernel(q_ref, kt_ref, vt_ref, kct_ref, vct_ref, tab_ref, o_ref, *, n_rows):
    n_chunks = n_rows // ROWS_PER_CHUNK
    chunk_q = ROWS_PER_CHUNK * GRID_W
    union = UNION_ROWS * GRID_W
    items = [(c, hh) for c in range(n_chunks) for hh in range(HEADS_PER_STEP)]
    done = {}

    def key_cols(c):
        first_key_row = min(max(c * ROWS_PER_CHUNK - WIN_R // 2, 0), n_rows - UNION_ROWS)
        return slice(first_key_row * GRID_W, first_key_row * GRID_W + union)

    def score(i):
        c, hh = items[i]
        kind = 0 if c == 0 else (2 if c == n_chunks - 1 else 1)
        qc = q_ref[0, hh, c * chunk_q:(c + 1) * chunk_q, :]
        s_band = _dot(qc, kt_ref[0, 0, hh, :, key_cols(c)]) + tab_ref[0, hh, kind]
        s_ctx = _dot(qc, kct_ref[0, 0, hh].astype(BF16))
        return s_band, s_ctx

    def normalise(i, scores):
        s_band, s_ctx = scores
        m = jnp.maximum(jnp.max(s_band, axis=-1, keepdims=True),
                        jnp.max(s_ctx, axis=-1, keepdims=True))
        p_band = jnp.exp(s_band - m)
        p_ctx = jnp.exp(s_ctx - m)
        denom = jnp.sum(p_band, axis=-1, keepdims=True) + jnp.sum(p_ctx, axis=-1, keepdims=True)
        return p_band.astype(BF16), p_ctx.astype(BF16), denom

    def finish(i, prob):
        c, hh = items[i]
        p_band, p_ctx, denom = prob
        o = (_dot_nt(p_band, vt_ref[0, 0, hh, :, key_cols(c)])
             + _dot_nt(p_ctx, vct_ref[0, 0, hh].astype(BF16)))
        done[hh] = o / denom
        if hh == HEADS_PER_STEP - 1:
            o_ref[0, c * chunk_q:(c + 1) * chunk_q, :] = jnp.concatenate(
                [done.pop(h) for h in range(HEADS_PER_STEP)], axis=-1).astype(BF16)

    _software_pipeline(len(items), score, normalise, finish)


def _lat_attention(q, k_t, v_t, cache_k_t, cache_v_t, tables, layer):
    n_seq, _, seq_len, _ = q.shape
    past_len = cache_k_t.shape[4]
    n_rows = seq_len // GRID_W
    assert n_rows % ROWS_PER_CHUNK == 0 and n_rows >= 2 * UNION_ROWS
    hps = HEADS_PER_STEP
    return pl.pallas_call(
        functools.partial(_lat_attn_kernel, n_rows=n_rows),
        grid=(N_HEADS // hps, n_seq),
        in_specs=[
            pl.BlockSpec((1, hps, seq_len, HEAD_DIM), lambda p, b: (b, p, 0, 0)),
            pl.BlockSpec((1, 1, hps, HEAD_DIM, seq_len), lambda p, b: (b, 0, p, 0, 0)),
            pl.BlockSpec((1, 1, hps, HEAD_DIM, seq_len), lambda p, b: (b, 0, p, 0, 0)),
            pl.BlockSpec((1, 1, hps, HEAD_DIM, past_len), lambda p, b: (b, layer, p, 0, 0)),
            pl.BlockSpec((1, 1, hps, HEAD_DIM, past_len), lambda p, b: (b, layer, p, 0, 0)),
            pl.BlockSpec((1, hps) + tables.shape[2:], lambda p, b: (layer, p, 0, 0, 0)),
        ],
        out_specs=pl.BlockSpec((1, seq_len, hps * HEAD_DIM), lambda p, b: (b, 0, p)),
        out_shape=jax.ShapeDtypeStruct((n_seq, seq_len, ATTN_WIDTH), BF16),
        compiler_params=_params(2),
        name="attn_lat",
    )(q, k_t, v_t, cache_k_t, cache_v_t, tables)


def _layer_norm(y, g, b):
    mu = jnp.mean(y, axis=-1, keepdims=True)
    yc = y - mu
    var = jnp.mean(yc * yc, axis=-1, keepdims=True)
    return yc * lax.rsqrt(var + LN_EPS) * g + b


def _post_kernel(x_ref, attn_ref, z_ref, zprev_ref, znext_ref, bg_ref, ga_ref, gc_ref, mod_ref,
                 wap_ref, wcp_ref, wo_ref, w1_ref, w2_ref, cw_ref, cb_ref,
                 ln1g_ref, ln1b_ref, b1_ref, b2_ref, ln2g_ref, ln2b_ref, o_ref,
                 *, mod_row0, tiles_per_seq, seq_len):
    i = pl.program_id(0)
    row = _cond_row(i, mod_row0, tiles_per_seq)
    d = D_MODEL
    g1 = mod_ref[0, pl.ds(row, 1), 2 * d:3 * d]
    sh2 = mod_ref[0, pl.ds(row, 1), 3 * d:4 * d]
    sc2 = mod_ref[0, pl.ds(row, 1), 4 * d:5 * d]
    g2 = mod_ref[0, pl.ds(row, 1), 5 * d:6 * d]

    z = z_ref[...].astype(F32)
    prev_row = zprev_ref[...].astype(F32)[HALO_ROWS - 1:HALO_ROWS, :]
    next_row = znext_ref[...].astype(F32)[0:1, :]
    pos = lax.broadcasted_iota(jnp.int32, z.shape, 0)
    seq_pos = (i * TOKEN_TILE + pos) & (seq_len - 1)
    z_before = jnp.where(pos == 0, prev_row, pltpu.roll(z, 1, 0))
    z_before = jnp.where(seq_pos == 0, 0.0, z_before)
    z_after = jnp.where(pos == TOKEN_TILE - 1, next_row, pltpu.roll(z, TOKEN_TILE - 1, 0))
    z_after = jnp.where(seq_pos == seq_len - 1, 0.0, z_after)
    conv = z_before * cw_ref[0:1, :] + z * cw_ref[1:2, :] + z_after * cw_ref[2:3, :] + cb_ref[...]
    conv_in = (bg_ref[...].astype(F32) * conv).astype(BF16)

    def merge(rs):
        attn_p = _dot(attn_ref[rs, :], wap_ref[...])
        conv_p = _dot(conv_in[rs, :], wcp_ref[...])
        return (jax.nn.sigmoid(ga_ref[rs, :].astype(F32)) * attn_p
                + jax.nn.sigmoid(gc_ref[rs, :].astype(F32)) * conv_p).astype(BF16)

    def mix_norm(merged, rs):
        mix = _dot(merged, wo_ref[...])
        x1 = _layer_norm(ALPHA * x_ref[rs, :] + g1 * mix, ln1g_ref[...], ln1b_ref[...])
        return x1, (x1 * (1.0 + sc2) + sh2).astype(BF16)

    def mlp_chunk(h2, c):
        cols = slice(c * FF_CHUNK, (c + 1) * FF_CHUNK)
        hidden = jnp.maximum(_dot(h2, w1_ref[:, cols]) + b1_ref[:, cols], 0.0)
        return _dot((hidden * hidden).astype(BF16), w2_ref[cols, :])

    def finish(x1, f, rs):
        o_ref[rs, :] = _layer_norm(ALPHA * x1 + g2 * f, ln2g_ref[...], ln2b_ref[...])

    groups = [slice(r0, r0 + POST_ROW_GROUP) for r0 in range(0, TOKEN_TILE, POST_ROW_GROUP)]
    n_chunks = D_FF // FF_CHUNK
    x1, h2 = mix_norm(merge(groups[0]), groups[0])
    f = b2_ref[...] + mlp_chunk(h2, 0)
    for n, rs in enumerate(groups):
        nxt = groups[n + 1] if n + 1 < len(groups) else None
        if nxt is not None:
            merged_next = merge(nxt)
        f = f + mlp_chunk(h2, 1)
        if nxt is not None:
            x1_next, h2_next = mix_norm(merged_next, nxt)
        for c in range(2, n_chunks):
            f = f + mlp_chunk(h2, c)
        if nxt is not None:
            f_next = b2_ref[...] + mlp_chunk(h2_next, 0)
        finish(x1, f, rs)
        if nxt is not None:
            x1, h2, f = x1_next, h2_next, f_next


def _post(x, attn, z, bg, ga, gc, mod, weights, layer, *, seq_len, mod_row0):
    n_tok = x.shape[0]
    assert seq_len & (seq_len - 1) == 0
    tps = max(seq_len // TOKEN_TILE, 1)
    n_tiles = n_tok // TOKEN_TILE
    halo_per_tile = TOKEN_TILE // HALO_ROWS
    n_halo_blocks = n_tok // HALO_ROWS
    tok = lambda width: pl.BlockSpec((TOKEN_TILE, width), lambda i: (i, 0))
    in_specs = [
        tok(D_MODEL), tok(ATTN_WIDTH), tok(CONV_WIDTH),
        pl.BlockSpec((HALO_ROWS, CONV_WIDTH), lambda i: (jnp.maximum(i * halo_per_tile - 1, 0), 0)),
        pl.BlockSpec((HALO_ROWS, CONV_WIDTH),
                     lambda i: (jnp.minimum((i + 1) * halo_per_tile, n_halo_blocks - 1), 0)),
        tok(CONV_WIDTH), tok(D_MODEL), tok(D_MODEL),
        pl.BlockSpec((1, COND_ROWS, 6 * D_MODEL), lambda i: (layer, 0, 0)),
    ]
    per_layer = weights["matrices"] + [weights["conv_w"]] + weights["vectors"]
    in_specs += [_layer_resident(w.shape[1:], layer) for w in per_layer]
    return pl.pallas_call(
        functools.partial(_post_kernel, mod_row0=mod_row0, tiles_per_seq=tps, seq_len=seq_len),
        grid=(n_tiles,),
        in_specs=in_specs,
        out_specs=tok(D_MODEL),
        out_shape=jax.ShapeDtypeStruct((n_tok, D_MODEL), F32),
        compiler_params=_params(1),
        name="post_ctx" if mod_row0 == 0 else "post_lat",
    )(x, attn, z, z, z, bg, ga, gc, mod, *per_layer)


def kernel(x_prompt, x_sample, cache_k, cache_v, c, c_ctx, w_mod, b_mod, w_in, rpb, conv_w, conv_b,
           w_attn_proj, w_conv_proj, w_o, ln1_g, ln1_b, w1, b1, w2, b2, ln2_g, ln2_b):
    batch, seq, _ = x_prompt.shape
    dec_batch, dec_seq, _ = x_sample.shape

    cond = jnp.concatenate(
        [c_ctx[None, :], c, jnp.zeros((COND_ROWS - 1 - dec_batch, D_MODEL), F32)], axis=0)
    mod = _modulation(cond, w_mod, b_mod)
    tables = _bias_tables(rpb)

    w_in_b = w_in.astype(BF16)
    w_kv_t = _transposed_kv_weights(w_in)
    cache_k_t = jnp.swapaxes(cache_k, -1, -2)
    cache_v_t = jnp.swapaxes(cache_v, -1, -2)
    weights = {
        "matrices": [w.astype(BF16) for w in (w_attn_proj, w_conv_proj, w_o, w1, w2)],
        "conv_w": conv_w,
        "vectors": [v.reshape(DEPTH, 1, -1) for v in (conv_b, ln1_g, ln1_b, b1, b2, ln2_g, ln2_b)],
    }

    xp = x_prompt.reshape(batch * seq, D_MODEL)
    xs = x_sample.reshape(dec_batch * dec_seq, D_MODEL)
    kv = None
    for layer in range(DEPTH):
        q, new_k_t, new_v_t, z, bg, ga, gc = _inproj(
            xp, mod, w_in_b, w_kv_t, layer, n_seq=batch, seq_len=seq, mod_row0=0, kv_prev=kv)
        kv = (new_k_t, new_v_t)
        attn = _ctx_attention(q, new_k_t, new_v_t, layer)
        xp = _post(xp, attn, z, bg, ga, gc, mod, weights, layer, seq_len=seq, mod_row0=0)

        q, k_t, v_t, z, bg, ga, gc = _inproj(
            xs, mod, w_in_b, w_kv_t, layer, n_seq=dec_batch, seq_len=dec_seq, mod_row0=1)
        attn = _lat_attention(q, k_t, v_t, cache_k_t, cache_v_t, tables, layer)
        xs = _post(xs, attn.reshape(dec_batch * dec_seq, ATTN_WIDTH), z, bg, ga, gc, mod, weights,
                   layer, seq_len=dec_seq, mod_row0=1)

    new_k = jnp.swapaxes(kv[0], -1, -2)
    new_v = jnp.swapaxes(kv[1], -1, -2)
    return (xp.reshape(batch, seq, D_MODEL), xs.reshape(dec_batch, dec_seq, D_MODEL), new_k, new_v)
```

```python
import functools

import jax
import jax.numpy as jnp
from jax import lax
from jax.experimental import pallas as pl
from jax.experimental.pallas import tpu as pltpu

D_MODEL = 1024
DEPTH = 4
N_HEADS = 8
HEAD_DIM = 64
ATTN_WIDTH = N_HEADS * HEAD_DIM
CONV_WIDTH = D_MODEL // 2
D_FF = 4 * D_MODEL
GRID_W = 64
WIN_R = 8
WIN_C = 16
RPB_ROWS = 2 * WIN_R - 1
RPB_COLS = 2 * WIN_C - 1
IN_COLS = 3 * ATTN_WIDTH + 3 * CONV_WIDTH + 2 * D_MODEL
SCALE = HEAD_DIM ** -0.5
ALPHA = (2.0 * DEPTH) ** 0.25
LN_EPS = 1e-5

COND_ROWS = 8
TOKEN_TILE = 512
FF_CHUNK = 1024
POST_ROW_GROUP = 256
HALO_ROWS = 16
ROWS_PER_CHUNK = 4
VMEM_LIMIT_BYTES = 56 * 1024 * 1024

BF16 = jnp.bfloat16
F32 = jnp.float32
NT_DIMS = (((1,), (1,)), ((), ()))


def _dot(a, b):
    return jnp.dot(a, b, preferred_element_type=F32)


def _dot_nt(a, b):
    return lax.dot_general(a, b, NT_DIMS, preferred_element_type=F32)


def _layer_resident(shape, layer):
    zeros = (0,) * len(shape)
    return pl.BlockSpec((None,) + tuple(shape), lambda *_: (layer,) + zeros,
                        pipeline_mode=pl.Buffered(1))


def _cond_row(tile, mod_row0, tiles_per_seq):
    if mod_row0 == 0:
        return 0
    return mod_row0 + tile // tiles_per_seq


def _params(n_grid_dims):
    return pltpu.CompilerParams(
        dimension_semantics=("arbitrary",) * n_grid_dims,
        vmem_limit_bytes=VMEM_LIMIT_BYTES,
    )


MOD_COL_TILE = 1536


def _mod_kernel(cond_ref, w_ref, b_ref, o_ref):
    c = cond_ref[...]
    s = (c * jax.nn.sigmoid(c)).astype(BF16)
    o_ref[0] = _dot(s, w_ref[0].astype(BF16)) + b_ref[0]


def _modulation(cond, w_mod, b_mod):
    n_cols = 6 * D_MODEL
    return pl.pallas_call(
        _mod_kernel,
        grid=(DEPTH, n_cols // MOD_COL_TILE),
        in_specs=[
            pl.BlockSpec((COND_ROWS, D_MODEL), lambda l, j: (0, 0)),
            pl.BlockSpec((1, D_MODEL, MOD_COL_TILE), lambda l, j: (l, 0, j)),
            pl.BlockSpec((1, 1, MOD_COL_TILE), lambda l, j: (l, 0, j)),
        ],
        out_specs=pl.BlockSpec((1, COND_ROWS, MOD_COL_TILE), lambda l, j: (l, 0, j)),
        out_shape=jax.ShapeDtypeStruct((DEPTH, COND_ROWS, n_cols), F32),
        compiler_params=_params(2),
        name="modulation",
    )(cond, w_mod, b_mod.reshape(DEPTH, 1, n_cols))


UNION_ROWS = WIN_R + ROWS_PER_CHUNK
N_CHUNK_KINDS = 3
assert ROWS_PER_CHUNK == WIN_R // 2


def _bias_table_kernel(rpb_ref, o_ref):
    l = pl.program_id(0)
    h = pl.program_id(1)
    base = (l * N_HEADS + h) * (RPB_ROWS * RPB_COLS)
    shape = (GRID_W, 2 * GRID_W)
    q = lax.broadcasted_iota(jnp.int32, shape, 0)
    lane = lax.broadcasted_iota(jnp.int32, shape, 1)
    kc = lane & (GRID_W - 1)
    t = kc - q + (WIN_C - 1)
    col_start = jnp.clip(q - WIN_C // 2, 0, GRID_W - WIN_C)
    valid = (kc >= col_start) & (kc < col_start + WIN_C)
    neg_inf = jnp.full(shape, -jnp.inf, F32)
    per_row = []
    for j in range(RPB_ROWS):
        acc = neg_inf
        for m in range(RPB_COLS):
            acc = jnp.where(t == m, rpb_ref[base + j * RPB_COLS + m], acc)
        per_row.append(jnp.where(valid, acc, neg_inf))
    chunk_kinds = (
        [(0, g) for g in range(ROWS_PER_CHUNK)],
        [(g, WIN_R // 2) for g in range(ROWS_PER_CHUNK)],
        [(ROWS_PER_CHUNK, WIN_R // 2 + g) for g in range(ROWS_PER_CHUNK)],
    )
    for kind, rows in enumerate(chunk_kinds):
        for g, (offset, d) in enumerate(rows):
            for pair in range(UNION_ROWS // 2):
                halves = []
                for j in (2 * pair, 2 * pair + 1):
                    i = j - offset
                    halves.append(per_row[i + WIN_R - 1 - d] if 0 <= i < WIN_R else neg_inf)
                o_ref[0, 0, kind, g * GRID_W:(g + 1) * GRID_W, pair * 128:(pair + 1) * 128] = (
                    jnp.where(lane < GRID_W, halves[0], halves[1]))


def _bias_tables(rpb):
    blk = (1, 1, N_CHUNK_KINDS, ROWS_PER_CHUNK * GRID_W, UNION_ROWS * GRID_W)
    return pl.pallas_call(
        _bias_table_kernel,
        grid=(DEPTH, N_HEADS),
        in_specs=[pl.BlockSpec(memory_space=pltpu.SMEM)],
        out_specs=pl.BlockSpec(blk, lambda l, h: (l, h, 0, 0, 0)),
        out_shape=jax.ShapeDtypeStruct((DEPTH, N_HEADS) + blk[2:], F32),
        compiler_params=_params(2),
        name="bias_tables",
    )(rpb.reshape(-1))


def _transpose_cast_kernel(w_ref, o_ref):
    o_ref[...] = w_ref[...].T.astype(BF16)


def _transposed_kv_weights(w_in):
    return pl.pallas_call(
        _transpose_cast_kernel,
        grid=(DEPTH, 2),
        in_specs=[pl.BlockSpec((None, D_MODEL, ATTN_WIDTH), lambda l, j: (l, 0, 1 + j))],
        out_specs=pl.BlockSpec((None, ATTN_WIDTH, D_MODEL), lambda l, j: (l, j, 0)),
        out_shape=jax.ShapeDtypeStruct((DEPTH, 2 * ATTN_WIDTH, D_MODEL), BF16),
        compiler_params=_params(2),
        name="kv_weight_transpose",
    )(w_in)


def _cast_rows(src_refs, dst_refs):
    for src, dst in zip(src_refs, dst_refs, strict=True):
        dst[...] = src[...].astype(BF16)


def _row_chunk_specs(stacked, layer, n_steps):
    in_specs, out_specs, out_shapes = [], [], []
    for w in stacked:
        _, n_rows, n_cols = w.shape
        chunk = n_rows // n_steps
        assert chunk * n_steps == n_rows and chunk % 16 == 0
        in_specs.append(pl.BlockSpec((None, chunk, n_cols), lambda i: (layer, i, 0)))
        out_specs.append(pl.BlockSpec((chunk, n_cols), lambda i: (i, 0)))
        out_shapes.append(jax.ShapeDtypeStruct((n_rows, n_cols), BF16))
    return in_specs, out_specs, out_shapes


def _resident(shape):
    zeros = (0,) * len(shape)
    return pl.BlockSpec(tuple(shape), lambda *_: zeros, pipeline_mode=pl.Buffered(1))


def _inproj_kernel(x_ref, mod_ref, w_ref, wkv_t_ref, *rest, mod_row0, tiles_per_seq, seqs_per_tile,
                   n_prev, n_cast):
    cast_src = rest[n_prev:n_prev + n_cast]
    outs = rest[n_prev + n_cast:]
    q_ref, kt_ref, vt_ref, z_ref, bg_ref, ga_ref, gc_ref = outs[:7]
    _cast_rows(cast_src, outs[7:])
    i = pl.program_id(0)
    row = _cond_row(i, mod_row0, tiles_per_seq)
    sh1 = mod_ref[0, pl.ds(row, 1), 0:D_MODEL]
    sc1 = mod_ref[0, pl.ds(row, 1), D_MODEL:2 * D_MODEL]
    h = (x_ref[...] * (1.0 + sc1) + sh1).astype(BF16)

    def proj(lo, hi):
        return _dot(h, w_ref[:, lo:hi])

    a = ATTN_WIDTH
    c0 = 3 * ATTN_WIDTH
    g0 = c0 + 3 * CONV_WIDTH
    ga_ref[...] = proj(g0, g0 + D_MODEL).astype(BF16)
    gc_ref[...] = proj(g0 + D_MODEL, g0 + 2 * D_MODEL).astype(BF16)
    rows = TOKEN_TILE // seqs_per_tile
    q = proj(0, a) * SCALE
    for s in range(seqs_per_tile):
        for hd in range(N_HEADS):
            q_ref[s, hd] = q[s * rows:(s + 1) * rows, hd * HEAD_DIM:(hd + 1) * HEAD_DIM].astype(BF16)
    for lo, ref in ((0, kt_ref), (a, vt_ref)):
        y_t = _dot_nt(wkv_t_ref[lo:lo + a, :], h)
        for s in range(seqs_per_tile):
            for hd in range(N_HEADS):
                ref[s, 0, hd] = y_t[hd * HEAD_DIM:(hd + 1) * HEAD_DIM,
                                    s * rows:(s + 1) * rows].astype(ref.dtype)
    u = proj(c0, c0 + CONV_WIDTH)
    cg = proj(c0 + 2 * CONV_WIDTH, c0 + 3 * CONV_WIDTH)
    z_ref[...] = (cg * u).astype(BF16)
    bg_ref[...] = proj(c0 + CONV_WIDTH, c0 + 2 * CONV_WIDTH).astype(BF16)


def _inproj(x, mod, w_in_l, w_kv_t, layer, *, n_seq, seq_len, mod_row0, kv_prev=None, cast_src=()):
    n_tok = n_seq * seq_len
    n_tiles = n_tok // TOKEN_TILE
    is_ctx = mod_row0 == 0
    if seq_len >= TOKEN_TILE:
        tps, spt, rows = seq_len // TOKEN_TILE, 1, TOKEN_TILE
        head_map = lambda i: (i // tps, 0, i % tps, 0)
        kv_map = lambda i: (i // tps, layer if is_ctx else 0, 0, 0, i % tps)
    else:
        tps, spt, rows = 1, TOKEN_TILE // seq_len, seq_len
        head_map = lambda i: (i, 0, 0, 0)
        kv_map = lambda i: (i, layer if is_ctx else 0, 0, 0, 0)
    head_blk = (spt, N_HEADS, rows, HEAD_DIM)
    kv_blk = (spt, 1, N_HEADS, HEAD_DIM, rows)
    if is_ctx:
        kv_shape = jax.ShapeDtypeStruct((n_seq, DEPTH, N_HEADS, HEAD_DIM, seq_len), F32)
    else:
        kv_shape = jax.ShapeDtypeStruct((n_seq, 1, N_HEADS, HEAD_DIM, seq_len), BF16)
    tok = lambda width: pl.BlockSpec((TOKEN_TILE, width), lambda i: (i, 0))
    tok_shape = lambda width: jax.ShapeDtypeStruct((n_tok, width), BF16)
    in_specs = [
        tok(D_MODEL),
        pl.BlockSpec((1, COND_ROWS, 6 * D_MODEL), lambda i: (layer, 0, 0)),
        _resident((D_MODEL, IN_COLS)),
        _layer_resident((2 * ATTN_WIDTH, D_MODEL), layer),
    ]
    args = [x, mod, w_in_l, w_kv_t]
    aliases = {}
    n_prev = 0
    if kv_prev is not None:
        in_specs += [pl.BlockSpec(memory_space=pl.ANY)] * 2
        args += list(kv_prev)
        aliases = {4: 1, 5: 2}
        n_prev = 2
    cast_in, cast_out, cast_shapes = _row_chunk_specs(cast_src, layer, n_tiles)
    args += list(cast_src)
    return pl.pallas_call(
        functools.partial(_inproj_kernel, mod_row0=mod_row0, tiles_per_seq=tps, seqs_per_tile=spt,
                          n_prev=n_prev, n_cast=len(cast_src)),
        grid=(n_tiles,),
        in_specs=in_specs + cast_in,
        out_specs=[
            pl.BlockSpec(head_blk, head_map),
            pl.BlockSpec(kv_blk, kv_map),
            pl.BlockSpec(kv_blk, kv_map),
            tok(CONV_WIDTH), tok(CONV_WIDTH), tok(D_MODEL), tok(D_MODEL),
        ] + cast_out,
        out_shape=[
            jax.ShapeDtypeStruct((n_seq, N_HEADS, seq_len, HEAD_DIM), BF16),
            kv_shape, kv_shape,
            tok_shape(CONV_WIDTH), tok_shape(CONV_WIDTH), tok_shape(D_MODEL), tok_shape(D_MODEL),
        ] + cast_shapes,
        input_output_aliases=aliases,
        compiler_params=_params(1),
        name="inproj_ctx" if is_ctx else "inproj_lat",
    )(*args)


def _software_pipeline(n_items, score, normalise, finish):
    scores = {0: score(0)}
    probs = {}
    for i in range(n_items + 1):
        if i + 1 < n_items:
            scores[i + 1] = score(i + 1)
        if i < n_items:
            probs[i] = normalise(i, scores.pop(i))
        if i >= 1:
            finish(i - 1, probs.pop(i - 1))


CTX_SEQS_PER_STEP = 4


def _ctx_attn_kernel(q_ref, kt_ref, vt_ref, o_ref):
    seq_len = q_ref.shape[2]
    items = [(s, hd) for s in range(CTX_SEQS_PER_STEP) for hd in range(N_HEADS)]
    done = {}

    def score(i):
        s, hd = items[i]
        return _dot(q_ref[s, hd], kt_ref[s, 0, hd].astype(BF16))

    def normalise(i, sc):
        p = jnp.exp(sc - jnp.max(sc, axis=-1, keepdims=True))
        return p.astype(BF16), jnp.sum(p, axis=-1, keepdims=True)

    def finish(i, prob):
        s, hd = items[i]
        p, denom = prob
        done[hd] = _dot_nt(p, vt_ref[s, 0, hd].astype(BF16)) / denom
        if hd % 2 == 1:
            o_ref[s * seq_len:(s + 1) * seq_len, (hd - 1) * HEAD_DIM:(hd + 1) * HEAD_DIM] = (
                jnp.concatenate([done.pop(hd - 1), done.pop(hd)], axis=-1).astype(BF16))

    _software_pipeline(len(items), score, normalise, finish)


def _ctx_attention(q, new_k, new_v, layer):
    n_seq, _, seq_len, _ = q.shape
    sps = CTX_SEQS_PER_STEP
    kv_blk = (sps, 1, N_HEADS, HEAD_DIM, seq_len)
    kv_map = lambda b: (b, layer, 0, 0, 0)
    return pl.pallas_call(
        _ctx_attn_kernel,
        grid=(n_seq // sps,),
        in_specs=[
            pl.BlockSpec((sps, N_HEADS, seq_len, HEAD_DIM), lambda b: (b, 0, 0, 0)),
            pl.BlockSpec(kv_blk, kv_map),
            pl.BlockSpec(kv_blk, kv_map),
        ],
        out_specs=pl.BlockSpec((sps * seq_len, ATTN_WIDTH), lambda b: (b, 0)),
        out_shape=jax.ShapeDtypeStruct((n_seq * seq_len, ATTN_WIDTH), BF16),
        compiler_params=_params(1),
        name="attn_ctx",
    )(q, new_k, new_v)


HEADS_PER_STEP = 2


def _lat_attn_kernel(q_ref, kt_ref, vt_ref, kct_ref, vct_ref, tab_ref, o_ref, *, n_rows):
    n_chunks = n_rows // ROWS_PER_CHUNK
    chunk_q = ROWS_PER_CHUNK * GRID_W
    union = UNION_ROWS * GRID_W
    items = [(c, hh) for c in range(n_chunks) for hh in range(HEADS_PER_STEP)]
    done = {}

    def key_cols(c):
        first_key_row = min(max(c * ROWS_PER_CHUNK - WIN_R // 2, 0), n_rows - UNION_ROWS)
        return slice(first_key_row * GRID_W, first_key_row * GRID_W + union)

    def score(i):
        c, hh = items[i]
        kind = 0 if c == 0 else (2 if c == n_chunks - 1 else 1)
        qc = q_ref[0, hh, c * chunk_q:(c + 1) * chunk_q, :]
        s_band = _dot(qc, kt_ref[0, 0, hh, :, key_cols(c)]) + tab_ref[0, hh, kind]
        s_ctx = _dot(qc, kct_ref[0, 0, hh].astype(BF16))
        return s_band, s_ctx

    def normalise(i, scores):
        s_band, s_ctx = scores
        m = jnp.maximum(jnp.max(s_band, axis=-1, keepdims=True),
                        jnp.max(s_ctx, axis=-1, keepdims=True))
        p_band = jnp.exp(s_band - m)
        p_ctx = jnp.exp(s_ctx - m)
        denom = jnp.sum(p_band, axis=-1, keepdims=True) + jnp.sum(p_ctx, axis=-1, keepdims=True)
        return p_band.astype(BF16), p_ctx.astype(BF16), denom

    def finish(i, prob):
        c, hh = items[i]
        p_band, p_ctx, denom = prob
        o = (_dot_nt(p_band, vt_ref[0, 0, hh, :, key_cols(c)])
             + _dot_nt(p_ctx, vct_ref[0, 0, hh].astype(BF16)))
        done[hh] = o / denom
        if hh == HEADS_PER_STEP - 1:
            o_ref[0, c * chunk_q:(c + 1) * chunk_q, :] = jnp.concatenate(
                [done.pop(h) for h in range(HEADS_PER_STEP)], axis=-1).astype(BF16)

    _software_pipeline(len(items), score, normalise, finish)


def _lat_attention(q, k_t, v_t, cache_k_t, cache_v_t, tables, layer):
    n_seq, _, seq_len, _ = q.shape
    past_len = cache_k_t.shape[4]
    n_rows = seq_len // GRID_W
    assert n_rows % ROWS_PER_CHUNK == 0 and n_rows >= 2 * UNION_ROWS
    hps = HEADS_PER_STEP
    return pl.pallas_call(
        functools.partial(_lat_attn_kernel, n_rows=n_rows),
        grid=(N_HEADS // hps, n_seq),
        in_specs=[
            pl.BlockSpec((1, hps, seq_len, HEAD_DIM), lambda p, b: (b, p, 0, 0)),
            pl.BlockSpec((1, 1, hps, HEAD_DIM, seq_len), lambda p, b: (b, 0, p, 0, 0)),
            pl.BlockSpec((1, 1, hps, HEAD_DIM, seq_len), lambda p, b: (b, 0, p, 0, 0)),
            pl.BlockSpec((1, 1, hps, HEAD_DIM, past_len), lambda p, b: (b, layer, p, 0, 0)),
            pl.BlockSpec((1, 1, hps, HEAD_DIM, past_len), lambda p, b: (b, layer, p, 0, 0)),
            pl.BlockSpec((1, hps) + tables.shape[2:], lambda p, b: (layer, p, 0, 0, 0)),
        ],
        out_specs=pl.BlockSpec((1, seq_len, hps * HEAD_DIM), lambda p, b: (b, 0, p)),
        out_shape=jax.ShapeDtypeStruct((n_seq, seq_len, ATTN_WIDTH), BF16),
        compiler_params=_params(2),
        name="attn_lat",
    )(q, k_t, v_t, cache_k_t, cache_v_t, tables)


def _layer_norm(y, g, b):
    mu = jnp.mean(y, axis=-1, keepdims=True)
    yc = y - mu
    var = jnp.mean(yc * yc, axis=-1, keepdims=True)
    return yc * lax.rsqrt(var + LN_EPS) * g + b


def _post_kernel(x_ref, attn_ref, z_ref, zprev_ref, znext_ref, bg_ref, ga_ref, gc_ref, mod_ref,
                 wap_ref, wcp_ref, wo_ref, w1_ref, w2_ref, cw_ref, cb_ref,
                 ln1g_ref, ln1b_ref, b1_ref, b2_ref, ln2g_ref, ln2b_ref, *rest,
                 mod_row0, tiles_per_seq, seq_len, n_cast):
    o_ref = rest[n_cast]
    _cast_rows(rest[:n_cast], rest[n_cast + 1:])
    i = pl.program_id(0)
    row = _cond_row(i, mod_row0, tiles_per_seq)
    d = D_MODEL
    g1 = mod_ref[0, pl.ds(row, 1), 2 * d:3 * d]
    sh2 = mod_ref[0, pl.ds(row, 1), 3 * d:4 * d]
    sc2 = mod_ref[0, pl.ds(row, 1), 4 * d:5 * d]
    g2 = mod_ref[0, pl.ds(row, 1), 5 * d:6 * d]

    z = z_ref[...].astype(F32)
    prev_row = zprev_ref[...].astype(F32)[HALO_ROWS - 1:HALO_ROWS, :]
    next_row = znext_ref[...].astype(F32)[0:1, :]
    pos = lax.broadcasted_iota(jnp.int32, z.shape, 0)
    seq_pos = (i * TOKEN_TILE + pos) & (seq_len - 1)
    z_before = jnp.where(pos == 0, prev_row, pltpu.roll(z, 1, 0))
    z_before = jnp.where(seq_pos == 0, 0.0, z_before)
    z_after = jnp.where(pos == TOKEN_TILE - 1, next_row, pltpu.roll(z, TOKEN_TILE - 1, 0))
    z_after = jnp.where(seq_pos == seq_len - 1, 0.0, z_after)
    conv = z_before * cw_ref[0:1, :] + z * cw_ref[1:2, :] + z_after * cw_ref[2:3, :] + cb_ref[...]
    conv_in = (bg_ref[...].astype(F32) * conv).astype(BF16)

    def merge(rs):
        attn_p = _dot(attn_ref[rs, :], wap_ref[...])
        conv_p = _dot(conv_in[rs, :], wcp_ref[...])
        return (jax.nn.sigmoid(ga_ref[rs, :].astype(F32)) * attn_p
                + jax.nn.sigmoid(gc_ref[rs, :].astype(F32)) * conv_p).astype(BF16)

    def mix_norm(merged, rs):
        mix = _dot(merged, wo_ref[...])
        x1 = _layer_norm(ALPHA * x_ref[rs, :] + g1 * mix, ln1g_ref[...], ln1b_ref[...])
        return x1, (x1 * (1.0 + sc2) + sh2).astype(BF16)

    def mlp_chunk(h2, c):
        cols = slice(c * FF_CHUNK, (c + 1) * FF_CHUNK)
        hidden = jnp.maximum(_dot(h2, w1_ref[:, cols]) + b1_ref[:, cols], 0.0)
        return _dot((hidden * hidden).astype(BF16), w2_ref[cols, :])

    def finish(x1, f, rs):
        o_ref[rs, :] = _layer_norm(ALPHA * x1 + g2 * f, ln2g_ref[...], ln2b_ref[...])

    groups = [slice(r0, r0 + POST_ROW_GROUP) for r0 in range(0, TOKEN_TILE, POST_ROW_GROUP)]
    n_chunks = D_FF // FF_CHUNK
    x1, h2 = mix_norm(merge(groups[0]), groups[0])
    f = b2_ref[...] + mlp_chunk(h2, 0)
    for n, rs in enumerate(groups):
        nxt = groups[n + 1] if n + 1 < len(groups) else None
        if nxt is not None:
            merged_next = merge(nxt)
        f = f + mlp_chunk(h2, 1)
        if nxt is not None:
            x1_next, h2_next = mix_norm(merged_next, nxt)
        for c in range(2, n_chunks):
            f = f + mlp_chunk(h2, c)
        if nxt is not None:
            f_next = b2_ref[...] + mlp_chunk(h2_next, 0)
        finish(x1, f, rs)
        if nxt is not None:
            x1, h2, f = x1_next, h2_next, f_next


def _post(x, attn, z, bg, ga, gc, mod, mats, weights, layer, *, seq_len, mod_row0, cast_src=(),
          cast_layer=None):
    n_tok = x.shape[0]
    assert seq_len & (seq_len - 1) == 0
    tps = max(seq_len // TOKEN_TILE, 1)
    n_tiles = n_tok // TOKEN_TILE
    halo_per_tile = TOKEN_TILE // HALO_ROWS
    n_halo_blocks = n_tok // HALO_ROWS
    tok = lambda width: pl.BlockSpec((TOKEN_TILE, width), lambda i: (i, 0))
    in_specs = [
        tok(D_MODEL), tok(ATTN_WIDTH), tok(CONV_WIDTH),
        pl.BlockSpec((HALO_ROWS, CONV_WIDTH), lambda i: (jnp.maximum(i * halo_per_tile - 1, 0), 0)),
        pl.BlockSpec((HALO_ROWS, CONV_WIDTH),
                     lambda i: (jnp.minimum((i + 1) * halo_per_tile, n_halo_blocks - 1), 0)),
        tok(CONV_WIDTH), tok(D_MODEL), tok(D_MODEL),
        pl.BlockSpec((1, COND_ROWS, 6 * D_MODEL), lambda i: (layer, 0, 0)),
    ]
    per_layer = [weights["conv_w"]] + weights["vectors"]
    in_specs += [_resident(m.shape) for m in mats]
    in_specs += [_layer_resident(w.shape[1:], layer) for w in per_layer]
    cast_in, cast_out, cast_shapes = _row_chunk_specs(cast_src, cast_layer, n_tiles)
    return pl.pallas_call(
        functools.partial(_post_kernel, mod_row0=mod_row0, tiles_per_seq=tps, seq_len=seq_len,
                          n_cast=len(cast_src)),
        grid=(n_tiles,),
        in_specs=in_specs + cast_in,
        out_specs=[tok(D_MODEL)] + cast_out,
        out_shape=[jax.ShapeDtypeStruct((n_tok, D_MODEL), F32)] + cast_shapes,
        compiler_params=_params(1),
        name="post_ctx" if mod_row0 == 0 else "post_lat",
    )(x, attn, z, z, z, bg, ga, gc, mod, *mats, *per_layer, *cast_src)


def kernel(x_prompt, x_sample, cache_k, cache_v, c, c_ctx, w_mod, b_mod, w_in, rpb, conv_w, conv_b,
           w_attn_proj, w_conv_proj, w_o, ln1_g, ln1_b, w1, b1, w2, b2, ln2_g, ln2_b):
    batch, seq, _ = x_prompt.shape
    dec_batch, dec_seq, _ = x_sample.shape

    cond = jnp.concatenate(
        [c_ctx[None, :], c, jnp.zeros((COND_ROWS - 1 - dec_batch, D_MODEL), F32)], axis=0)
    mod = _modulation(cond, w_mod, b_mod)
    tables = _bias_tables(rpb)

    w_in_l = w_in[0].astype(BF16)
    post_mats_f32 = [w_attn_proj, w_conv_proj, w_o, w1, w2]
    w_kv_t = _transposed_kv_weights(w_in)
    cache_k_t = jnp.swapaxes(cache_k, -1, -2)
    cache_v_t = jnp.swapaxes(cache_v, -1, -2)
    weights = {
        "conv_w": conv_w,
        "vectors": [v.reshape(DEPTH, 1, -1) for v in (conv_b, ln1_g, ln1_b, b1, b2, ln2_g, ln2_b)],
    }

    xp = x_prompt.reshape(batch * seq, D_MODEL)
    xs = x_sample.reshape(dec_batch * dec_seq, D_MODEL)
    kv = None
    for layer in range(DEPTH):
        q, new_k_t, new_v_t, z, bg, ga, gc, *mats = _inproj(
            xp, mod, w_in_l, w_kv_t, layer, n_seq=batch, seq_len=seq, mod_row0=0, kv_prev=kv,
            cast_src=post_mats_f32)
        kv = (new_k_t, new_v_t)
        attn = _ctx_attention(q, new_k_t, new_v_t, layer)
        last = layer == DEPTH - 1
        xp, *w_in_next = _post(xp, attn, z, bg, ga, gc, mod, mats, weights, layer, seq_len=seq,
                               mod_row0=0, cast_src=() if last else (w_in,), cast_layer=layer + 1)

        q, k_t, v_t, z, bg, ga, gc = _inproj(
            xs, mod, w_in_l, w_kv_t, layer, n_seq=dec_batch, seq_len=dec_seq, mod_row0=1)
        attn = _lat_attention(q, k_t, v_t, cache_k_t, cache_v_t, tables, layer)
        (xs,) = _post(xs, attn.reshape(dec_batch * dec_seq, ATTN_WIDTH), z, bg, ga, gc, mod, mats,
                      weights, layer, seq_len=dec_seq, mod_row0=1)
        if not last:
            (w_in_l,) = w_in_next

    new_k = jnp.swapaxes(kv[0], -1, -2)
    new_v = jnp.swapaxes(kv[1], -1, -2)
    return (xp.reshape(batch, seq, D_MODEL), xs.reshape(dec_batch, dec_seq, D_MODEL), new_k, new_v)
```

```python
import functools

import jax
import jax.numpy as jnp
from jax import lax
from jax.experimental import pallas as pl
from jax.experimental.pallas import tpu as pltpu

D_MODEL = 1024
DEPTH = 4
N_HEADS = 8
HEAD_DIM = 64
ATTN_WIDTH = N_HEADS * HEAD_DIM
CONV_WIDTH = D_MODEL // 2
D_FF = 4 * D_MODEL
GRID_W = 64
WIN_R = 8
WIN_C = 16
RPB_ROWS = 2 * WIN_R - 1
RPB_COLS = 2 * WIN_C - 1
IN_COLS = 3 * ATTN_WIDTH + 3 * CONV_WIDTH + 2 * D_MODEL
SCALE = HEAD_DIM ** -0.5
ALPHA = (2.0 * DEPTH) ** 0.25
LN_EPS = 1e-5

COND_ROWS = 8
INPROJ_TILE = 512
TOKEN_TILE = 512
FF_CHUNK = 1024
POST_ROW_GROUP = 256
HALO_ROWS = 16
ROWS_PER_CHUNK = 4
VMEM_LIMIT_BYTES = 56 * 1024 * 1024

BF16 = jnp.bfloat16
F32 = jnp.float32
NT_DIMS = (((1,), (1,)), ((), ()))


def _dot(a, b):
    return jnp.dot(a, b, preferred_element_type=F32)


def _dot_nt(a, b):
    return lax.dot_general(a, b, NT_DIMS, preferred_element_type=F32)


def _resident(shape):
    zeros = (0,) * len(shape)
    return pl.BlockSpec(tuple(shape), lambda *_: zeros, pipeline_mode=pl.Buffered(1))


def _layer_resident(shape, layer):
    zeros = (0,) * len(shape)
    return pl.BlockSpec((None,) + tuple(shape), lambda *_: (layer,) + zeros,
                        pipeline_mode=pl.Buffered(1))


def _cond_row(tile, mod_row0, tiles_per_seq):
    if mod_row0 == 0:
        return 0
    return mod_row0 + tile // tiles_per_seq


def _params(n_grid_dims):
    return pltpu.CompilerParams(
        dimension_semantics=("arbitrary",) * n_grid_dims,
        vmem_limit_bytes=VMEM_LIMIT_BYTES,
    )


def _cast_rows(src_refs, dst_refs):
    for src, dst in zip(src_refs, dst_refs, strict=True):
        dst[...] = src[...].astype(BF16)


def _row_chunk_specs(stacked, layer, n_steps):
    in_specs, out_specs, out_shapes = [], [], []
    for w in stacked:
        _, n_rows, n_cols = w.shape
        chunk = n_rows // n_steps
        assert chunk * n_steps == n_rows and chunk % 16 == 0
        in_specs.append(pl.BlockSpec((None, chunk, n_cols), lambda i: (layer, i, 0)))
        out_specs.append(pl.BlockSpec((chunk, n_cols), lambda i: (i, 0)))
        out_shapes.append(jax.ShapeDtypeStruct((n_rows, n_cols), BF16))
    return in_specs, out_specs, out_shapes


MOD_COL_TILE = 1536


def _mod_kernel(cond_ref, w_ref, b_ref, o_ref):
    c = cond_ref[...]
    s = (c * jax.nn.sigmoid(c)).astype(BF16)
    o_ref[0] = _dot(s, w_ref[0].astype(BF16)) + b_ref[0]


def _modulation(cond, w_mod, b_mod):
    n_cols = 6 * D_MODEL
    return pl.pallas_call(
        _mod_kernel,
        grid=(DEPTH, n_cols // MOD_COL_TILE),
        in_specs=[
            pl.BlockSpec((COND_ROWS, D_MODEL), lambda l, j: (0, 0)),
            pl.BlockSpec((1, D_MODEL, MOD_COL_TILE), lambda l, j: (l, 0, j)),
            pl.BlockSpec((1, 1, MOD_COL_TILE), lambda l, j: (l, 0, j)),
        ],
        out_specs=pl.BlockSpec((1, COND_ROWS, MOD_COL_TILE), lambda l, j: (l, 0, j)),
        out_shape=jax.ShapeDtypeStruct((DEPTH, COND_ROWS, n_cols), F32),
        compiler_params=_params(2),
        name="modulation",
    )(cond, w_mod, b_mod.reshape(DEPTH, 1, n_cols))


UNION_ROWS = WIN_R + ROWS_PER_CHUNK
N_CHUNK_KINDS = 3
assert ROWS_PER_CHUNK == WIN_R // 2


def _bias_table_kernel(rpb_ref, o_ref):
    l = pl.program_id(0)
    h = pl.program_id(1)
    base = (l * N_HEADS + h) * (RPB_ROWS * RPB_COLS)
    shape = (GRID_W, 2 * GRID_W)
    q = lax.broadcasted_iota(jnp.int32, shape, 0)
    lane = lax.broadcasted_iota(jnp.int32, shape, 1)
    kc = lane & (GRID_W - 1)
    col_start = jnp.clip(q - WIN_C // 2, 0, GRID_W - WIN_C)
    valid = (kc >= col_start) & (kc < col_start + WIN_C)
    neg_inf = jnp.full(shape, -jnp.inf, F32)
    sub = lax.broadcasted_iota(jnp.int32, (8, 2 * GRID_W), 0)
    lane8 = lax.broadcasted_iota(jnp.int32, (8, 2 * GRID_W), 1)
    pad = GRID_W // 2
    t = lane8 - pad - sub + (WIN_C - 1)
    per_row = []
    for j in range(RPB_ROWS):
        tile = jnp.full((8, 2 * GRID_W), -jnp.inf, F32)
        for m in range(RPB_COLS):
            tile = jnp.where(t == m, rpb_ref[base + j * RPB_COLS + m], tile)
        groups = []
        for a in range(GRID_W // 8):
            low = pltpu.roll(tile, (8 * a - pad) % (2 * GRID_W), 1)
            high = pltpu.roll(tile, 8 * a - pad + GRID_W, 1)
            groups.append(jnp.where(lane8 < GRID_W, low, high))
        per_row.append(jnp.where(valid, jnp.concatenate(groups, axis=0), neg_inf))
    chunk_kinds = (
        [(0, g) for g in range(ROWS_PER_CHUNK)],
        [(g, WIN_R // 2) for g in range(ROWS_PER_CHUNK)],
        [(ROWS_PER_CHUNK, WIN_R // 2 + g) for g in range(ROWS_PER_CHUNK)],
    )
    for kind, rows in enumerate(chunk_kinds):
        for g, (offset, d) in enumerate(rows):
            for pair in range(UNION_ROWS // 2):
                halves = []
                for j in (2 * pair, 2 * pair + 1):
                    i = j - offset
                    halves.append(per_row[i + WIN_R - 1 - d] if 0 <= i < WIN_R else neg_inf)
                o_ref[0, 0, kind, g * GRID_W:(g + 1) * GRID_W, pair * 128:(pair + 1) * 128] = (
                    jnp.where(lane < GRID_W, halves[0], halves[1]))


def _bias_tables(rpb):
    blk = (1, 1, N_CHUNK_KINDS, ROWS_PER_CHUNK * GRID_W, UNION_ROWS * GRID_W)
    return pl.pallas_call(
        _bias_table_kernel,
        grid=(DEPTH, N_HEADS),
        in_specs=[pl.BlockSpec(memory_space=pltpu.SMEM)],
        out_specs=pl.BlockSpec(blk, lambda l, h: (l, h, 0, 0, 0)),
        out_shape=jax.ShapeDtypeStruct((DEPTH, N_HEADS) + blk[2:], F32),
        compiler_params=_params(2),
        name="bias_tables",
    )(rpb.reshape(-1))


def _transpose_cast_kernel(w_ref, o_ref):
    o_ref[...] = w_ref[...].T.astype(BF16)


def _transposed_kv_weights(w_in):
    return pl.pallas_call(
        _transpose_cast_kernel,
        grid=(DEPTH, 2),
        in_specs=[pl.BlockSpec((None, D_MODEL, ATTN_WIDTH), lambda l, j: (l, 0, 1 + j))],
        out_specs=pl.BlockSpec((None, ATTN_WIDTH, D_MODEL), lambda l, j: (l, j, 0)),
        out_shape=jax.ShapeDtypeStruct((DEPTH, 2 * ATTN_WIDTH, D_MODEL), BF16),
        compiler_params=_params(2),
        name="kv_weight_transpose",
    )(w_in)


def _inproj_kernel(x_ref, mod_ref, w_ref, wkv_t_ref, *rest, mod_row0, tiles_per_seq, seqs_per_tile,
                   n_prev, n_cast):
    cast_src = rest[n_prev:n_prev + n_cast]
    outs = rest[n_prev + n_cast:]
    q_ref, kt_ref, vt_ref, z_ref, bg_ref, ga_ref, gc_ref = outs[:7]
    _cast_rows(cast_src, outs[7:])
    i = pl.program_id(0)
    row = _cond_row(i, mod_row0, tiles_per_seq)
    sh1 = mod_ref[0, pl.ds(row, 1), 0:D_MODEL]
    sc1 = mod_ref[0, pl.ds(row, 1), D_MODEL:2 * D_MODEL]
    h = (x_ref[...] * (1.0 + sc1) + sh1).astype(BF16)

    def proj(lo, hi):
        return _dot(h, w_ref[:, lo:hi])

    a = ATTN_WIDTH
    c0 = 3 * ATTN_WIDTH
    g0 = c0 + 3 * CONV_WIDTH
    ga_ref[...] = proj(g0, g0 + D_MODEL).astype(BF16)
    gc_ref[...] = proj(g0 + D_MODEL, g0 + 2 * D_MODEL).astype(BF16)
    rows = INPROJ_TILE // seqs_per_tile
    q = proj(0, a) * SCALE
    for s in range(seqs_per_tile):
        for hd in range(N_HEADS):
            q_ref[s, hd] = q[s * rows:(s + 1) * rows, hd * HEAD_DIM:(hd + 1) * HEAD_DIM].astype(BF16)
    for lo, ref in ((0, kt_ref), (a, vt_ref)):
        y_t = _dot_nt(wkv_t_ref[lo:lo + a, :], h)
        for s in range(seqs_per_tile):
            for hd in range(N_HEADS):
                ref[s, 0, hd] = y_t[hd * HEAD_DIM:(hd + 1) * HEAD_DIM,
                                    s * rows:(s + 1) * rows].astype(ref.dtype)
    u = proj(c0, c0 + CONV_WIDTH)
    cg = proj(c0 + 2 * CONV_WIDTH, c0 + 3 * CONV_WIDTH)
    z_ref[...] = (cg * u).astype(BF16)
    bg_ref[...] = proj(c0 + CONV_WIDTH, c0 + 2 * CONV_WIDTH).astype(BF16)


def _inproj(x, mod, w_in_l, w_kv_t, layer, *, n_seq, seq_len, mod_row0, kv_prev=None, cast_src=()):
    n_tok = n_seq * seq_len
    n_tiles = n_tok // INPROJ_TILE
    is_ctx = mod_row0 == 0
    if seq_len >= INPROJ_TILE:
        tps, spt, rows = seq_len // INPROJ_TILE, 1, INPROJ_TILE
        head_map = lambda i: (i // tps, 0, i % tps, 0)
        kv_map = lambda i: (i // tps, layer if is_ctx else 0, 0, 0, i % tps)
    else:
        tps, spt, rows = 1, INPROJ_TILE // seq_len, seq_len
        head_map = lambda i: (i, 0, 0, 0)
        kv_map = lambda i: (i, layer if is_ctx else 0, 0, 0, 0)
    head_blk = (spt, N_HEADS, rows, HEAD_DIM)
    kv_blk = (spt, 1, N_HEADS, HEAD_DIM, rows)
    if is_ctx:
        kv_shape = jax.ShapeDtypeStruct((n_seq, DEPTH, N_HEADS, HEAD_DIM, seq_len), F32)
    else:
        kv_shape = jax.ShapeDtypeStruct((n_seq, 1, N_HEADS, HEAD_DIM, seq_len), BF16)
    tok = lambda width: pl.BlockSpec((INPROJ_TILE, width), lambda i: (i, 0))
    tok_shape = lambda width: jax.ShapeDtypeStruct((n_tok, width), BF16)
    in_specs = [
        tok(D_MODEL),
        pl.BlockSpec((1, COND_ROWS, 6 * D_MODEL), lambda i: (layer, 0, 0)),
        _resident((D_MODEL, IN_COLS)),
        _layer_resident((2 * ATTN_WIDTH, D_MODEL), layer),
    ]
    args = [x, mod, w_in_l, w_kv_t]
    aliases = {}
    n_prev = 0
    if kv_prev is not None:
        in_specs += [pl.BlockSpec(memory_space=pl.ANY)] * 2
        args += list(kv_prev)
        aliases = {4: 1, 5: 2}
        n_prev = 2
    cast_in, cast_out, cast_shapes = _row_chunk_specs(cast_src, layer, n_tiles)
    args += list(cast_src)
    return pl.pallas_call(
        functools.partial(_inproj_kernel, mod_row0=mod_row0, tiles_per_seq=tps, seqs_per_tile=spt,
                          n_prev=n_prev, n_cast=len(cast_src)),
        grid=(n_tiles,),
        in_specs=in_specs + cast_in,
        out_specs=[
            pl.BlockSpec(head_blk, head_map),
            pl.BlockSpec(kv_blk, kv_map),
            pl.BlockSpec(kv_blk, kv_map),
            tok(CONV_WIDTH), tok(CONV_WIDTH), tok(D_MODEL), tok(D_MODEL),
        ] + cast_out,
        out_shape=[
            jax.ShapeDtypeStruct((n_seq, N_HEADS, seq_len, HEAD_DIM), BF16),
            kv_shape, kv_shape,
            tok_shape(CONV_WIDTH), tok_shape(CONV_WIDTH), tok_shape(D_MODEL), tok_shape(D_MODEL),
        ] + cast_shapes,
        input_output_aliases=aliases,
        compiler_params=_params(1),
        name="inproj_ctx" if is_ctx else "inproj_lat",
    )(*args)


def _software_pipeline(n_items, score, normalise, finish):
    scores = {0: score(0)}
    probs = {}
    for i in range(n_items + 1):
        if i + 1 < n_items:
            scores[i + 1] = score(i + 1)
        if i < n_items:
            probs[i] = normalise(i, scores.pop(i))
        if i >= 1:
            finish(i - 1, probs.pop(i - 1))


CTX_SEQS_PER_STEP = 4


def _ctx_attn_kernel(q_ref, kt_ref, vt_ref, o_ref):
    seq_len = q_ref.shape[2]
    items = [(s, hd) for s in range(CTX_SEQS_PER_STEP) for hd in range(N_HEADS)]
    done = {}

    def score(i):
        s, hd = items[i]
        return _dot(q_ref[s, hd], kt_ref[s, 0, hd].astype(BF16))

    def normalise(i, sc):
        p = jnp.exp(sc - jnp.max(sc, axis=-1, keepdims=True))
        return p.astype(BF16), jnp.sum(p, axis=-1, keepdims=True)

    def finish(i, prob):
        s, hd = items[i]
        p, denom = prob
        done[hd] = _dot_nt(p, vt_ref[s, 0, hd].astype(BF16)) / denom
        if hd % 2 == 1:
            o_ref[s * seq_len:(s + 1) * seq_len, (hd - 1) * HEAD_DIM:(hd + 1) * HEAD_DIM] = (
                jnp.concatenate([done.pop(hd - 1), done.pop(hd)], axis=-1).astype(BF16))

    _software_pipeline(len(items), score, normalise, finish)


def _ctx_attention(q, new_k_t, new_v_t, layer):
    n_seq, _, seq_len, _ = q.shape
    sps = CTX_SEQS_PER_STEP
    kv_blk = (sps, 1, N_HEADS, HEAD_DIM, seq_len)
    kv_map = lambda b: (b, layer, 0, 0, 0)
    return pl.pallas_call(
        _ctx_attn_kernel,
        grid=(n_seq // sps,),
        in_specs=[
            pl.BlockSpec((sps, N_HEADS, seq_len, HEAD_DIM), lambda b: (b, 0, 0, 0)),
            pl.BlockSpec(kv_blk, kv_map),
            pl.BlockSpec(kv_blk, kv_map),
        ],
        out_specs=pl.BlockSpec((sps * seq_len, ATTN_WIDTH), lambda b: (b, 0)),
        out_shape=jax.ShapeDtypeStruct((n_seq * seq_len, ATTN_WIDTH), BF16),
        compiler_params=_params(1),
        name="attn_ctx",
    )(q, new_k_t, new_v_t)


HEADS_PER_STEP = 2


def _lat_attn_kernel(q_ref, kt_ref, vt_ref, kct_ref, vct_ref, tab_ref, o_ref, *, n_rows):
    n_chunks = n_rows // ROWS_PER_CHUNK
    chunk_q = ROWS_PER_CHUNK * GRID_W
    union = UNION_ROWS * GRID_W
    items = [(c, hh) for c in range(n_chunks) for hh in range(HEADS_PER_STEP)]
    done = {}

    def key_cols(c):
        first_key_row = min(max(c * ROWS_PER_CHUNK - WIN_R // 2, 0), n_rows - UNION_ROWS)
        return slice(first_key_row * GRID_W, first_key_row * GRID_W + union)

    def score(i):
        c, hh = items[i]
        kind = 0 if c == 0 else (2 if c == n_chunks - 1 else 1)
        qc = q_ref[0, hh, c * chunk_q:(c + 1) * chunk_q, :]
        s_band = _dot(qc, kt_ref[0, 0, hh, :, key_cols(c)]) + tab_ref[0, hh, kind]
        s_ctx = _dot(qc, kct_ref[0, 0, hh].astype(BF16))
        return s_band, s_ctx

    def normalise(i, scores):
        s_band, s_ctx = scores
        m = jnp.maximum(jnp.max(s_band, axis=-1, keepdims=True),
                        jnp.max(s_ctx, axis=-1, keepdims=True))
        p_band = jnp.exp(s_band - m)
        p_ctx = jnp.exp(s_ctx - m)
        denom = jnp.sum(p_band, axis=-1, keepdims=True) + jnp.sum(p_ctx, axis=-1, keepdims=True)
        return p_band.astype(BF16), p_ctx.astype(BF16), denom

    def finish(i, prob):
        c, hh = items[i]
        p_band, p_ctx, denom = prob
        o = (_dot_nt(p_band, vt_ref[0, 0, hh, :, key_cols(c)])
             + _dot_nt(p_ctx, vct_ref[0, 0, hh].astype(BF16)))
        done[hh] = o / denom
        if hh == HEADS_PER_STEP - 1:
            o_ref[0, c * chunk_q:(c + 1) * chunk_q, :] = jnp.concatenate(
                [done.pop(h) for h in range(HEADS_PER_STEP)], axis=-1).astype(BF16)

    _software_pipeline(len(items), score, normalise, finish)


def _lat_attention(q, k_t, v_t, cache_k_t, cache_v_t, tables, layer):
    n_seq, _, seq_len, _ = q.shape
    past_len = cache_k_t.shape[4]
    n_rows = seq_len // GRID_W
    assert n_rows % ROWS_PER_CHUNK == 0 and n_rows >= 2 * UNION_ROWS
    hps = HEADS_PER_STEP
    return pl.pallas_call(
        functools.partial(_lat_attn_kernel, n_rows=n_rows),
        grid=(N_HEADS // hps, n_seq),
        in_specs=[
            pl.BlockSpec((1, hps, seq_len, HEAD_DIM), lambda p, b: (b, p, 0, 0)),
            pl.BlockSpec((1, 1, hps, HEAD_DIM, seq_len), lambda p, b: (b, 0, p, 0, 0)),
            pl.BlockSpec((1, 1, hps, HEAD_DIM, seq_len), lambda p, b: (b, 0, p, 0, 0)),
            pl.BlockSpec((1, 1, hps, HEAD_DIM, past_len), lambda p, b: (b, layer, p, 0, 0)),
            pl.BlockSpec((1, 1, hps, HEAD_DIM, past_len), lambda p, b: (b, layer, p, 0, 0)),
            pl.BlockSpec((1, hps) + tables.shape[2:], lambda p, b: (layer, p, 0, 0, 0)),
        ],
        out_specs=pl.BlockSpec((1, seq_len, hps * HEAD_DIM), lambda p, b: (b, 0, p)),
        out_shape=jax.ShapeDtypeStruct((n_seq, seq_len, ATTN_WIDTH), BF16),
        compiler_params=_params(2),
        name="attn_lat",
    )(q, k_t, v_t, cache_k_t, cache_v_t, tables)


def _layer_norm(y, g, b):
    mu = jnp.mean(y, axis=-1, keepdims=True)
    yc = y - mu
    var = jnp.mean(yc * yc, axis=-1, keepdims=True)
    return yc * lax.rsqrt(var + LN_EPS) * g + b


def _post_kernel(x_ref, attn_ref, z_ref, zprev_ref, znext_ref, bg_ref, ga_ref, gc_ref, mod_ref,
                 wap_ref, wcp_ref, wo_ref, w1_ref, w2_ref, cw_ref, cb_ref,
                 ln1g_ref, ln1b_ref, b1_ref, b2_ref, ln2g_ref, ln2b_ref, *rest,
                 mod_row0, tiles_per_seq, seq_len, n_cast):
    o_ref = rest[n_cast]
    _cast_rows(rest[:n_cast], rest[n_cast + 1:])
    i = pl.program_id(0)
    row = _cond_row(i, mod_row0, tiles_per_seq)
    d = D_MODEL
    g1 = mod_ref[0, pl.ds(row, 1), 2 * d:3 * d]
    sh2 = mod_ref[0, pl.ds(row, 1), 3 * d:4 * d]
    sc2 = mod_ref[0, pl.ds(row, 1), 4 * d:5 * d]
    g2 = mod_ref[0, pl.ds(row, 1), 5 * d:6 * d]

    z = z_ref[...].astype(F32)
    prev_row = zprev_ref[...].astype(F32)[HALO_ROWS - 1:HALO_ROWS, :]
    next_row = znext_ref[...].astype(F32)[0:1, :]
    pos = lax.broadcasted_iota(jnp.int32, z.shape, 0)
    seq_pos = (i * TOKEN_TILE + pos) & (seq_len - 1)
    z_before = jnp.where(pos == 0, prev_row, pltpu.roll(z, 1, 0))
    z_before = jnp.where(seq_pos == 0, 0.0, z_before)
    z_after = jnp.where(pos == TOKEN_TILE - 1, next_row, pltpu.roll(z, TOKEN_TILE - 1, 0))
    z_after = jnp.where(seq_pos == seq_len - 1, 0.0, z_after)
    conv = z_before * cw_ref[0:1, :] + z * cw_ref[1:2, :] + z_after * cw_ref[2:3, :] + cb_ref[...]
    conv_in = (bg_ref[...].astype(F32) * conv).astype(BF16)

    def merge(rs):
        attn_p = _dot(attn_ref[rs, :], wap_ref[...])
        conv_p = _dot(conv_in[rs, :], wcp_ref[...])
        return (jax.nn.sigmoid(ga_ref[rs, :].astype(F32)) * attn_p
                + jax.nn.sigmoid(gc_ref[rs, :].astype(F32)) * conv_p).astype(BF16)

    def mix_norm(merged, rs):
        mix = _dot(merged, wo_ref[...])
        x1 = _layer_norm(ALPHA * x_ref[rs, :] + g1 * mix, ln1g_ref[...], ln1b_ref[...])
        return x1, (x1 * (1.0 + sc2) + sh2).astype(BF16)

    def mlp_chunk(h2, c):
        cols = slice(c * FF_CHUNK, (c + 1) * FF_CHUNK)
        hidden = jnp.maximum(_dot(h2, w1_ref[:, cols]) + b1_ref[:, cols], 0.0)
        return _dot((hidden * hidden).astype(BF16), w2_ref[cols, :])

    def finish(x1, f, rs):
        o_ref[rs, :] = _layer_norm(ALPHA * x1 + g2 * f, ln2g_ref[...], ln2b_ref[...])

    groups = [slice(r0, r0 + POST_ROW_GROUP) for r0 in range(0, TOKEN_TILE, POST_ROW_GROUP)]
    n_chunks = D_FF // FF_CHUNK
    x1, h2 = mix_norm(merge(groups[0]), groups[0])
    f = b2_ref[...] + mlp_chunk(h2, 0)
    for n, rs in enumerate(groups):
        nxt = groups[n + 1] if n + 1 < len(groups) else None
        if nxt is not None:
            merged_next = merge(nxt)
        f = f + mlp_chunk(h2, 1)
        if nxt is not None:
            x1_next, h2_next = mix_norm(merged_next, nxt)
        for c in range(2, n_chunks):
            f = f + mlp_chunk(h2, c)
        if nxt is not None:
            f_next = b2_ref[...] + mlp_chunk(h2_next, 0)
        finish(x1, f, rs)
        if nxt is not None:
            x1, h2, f = x1_next, h2_next, f_next


def _post(x, attn, z, bg, ga, gc, mod, mats, weights, layer, *, seq_len, mod_row0, cast_src=(),
          cast_layer=None):
    n_tok = x.shape[0]
    assert seq_len & (seq_len - 1) == 0
    tps = max(seq_len // TOKEN_TILE, 1)
    n_tiles = n_tok // TOKEN_TILE
    halo_per_tile = TOKEN_TILE // HALO_ROWS
    n_halo_blocks = n_tok // HALO_ROWS
    tok = lambda width: pl.BlockSpec((TOKEN_TILE, width), lambda i: (i, 0))
    in_specs = [
        tok(D_MODEL), tok(ATTN_WIDTH), tok(CONV_WIDTH),
        pl.BlockSpec((HALO_ROWS, CONV_WIDTH), lambda i: (jnp.maximum(i * halo_per_tile - 1, 0), 0)),
        pl.BlockSpec((HALO_ROWS, CONV_WIDTH),
                     lambda i: (jnp.minimum((i + 1) * halo_per_tile, n_halo_blocks - 1), 0)),
        tok(CONV_WIDTH), tok(D_MODEL), tok(D_MODEL),
        pl.BlockSpec((1, COND_ROWS, 6 * D_MODEL), lambda i: (layer, 0, 0)),
    ]
    per_layer = [weights["conv_w"]] + weights["vectors"]
    in_specs += [_resident(m.shape) for m in mats]
    in_specs += [_layer_resident(w.shape[1:], layer) for w in per_layer]
    cast_in, cast_out, cast_shapes = _row_chunk_specs(cast_src, cast_layer, n_tiles)
    return pl.pallas_call(
        functools.partial(_post_kernel, mod_row0=mod_row0, tiles_per_seq=tps, seq_len=seq_len,
                          n_cast=len(cast_src)),
        grid=(n_tiles,),
        in_specs=in_specs + cast_in,
        out_specs=[tok(D_MODEL)] + cast_out,
        out_shape=[jax.ShapeDtypeStruct((n_tok, D_MODEL), F32)] + cast_shapes,
        compiler_params=_params(1),
        name="post_ctx" if mod_row0 == 0 else "post_lat",
    )(x, attn, z, z, z, bg, ga, gc, mod, *mats, *per_layer, *cast_src)


def kernel(x_prompt, x_sample, cache_k, cache_v, c, c_ctx, w_mod, b_mod, w_in, rpb, conv_w, conv_b,
           w_attn_proj, w_conv_proj, w_o, ln1_g, ln1_b, w1, b1, w2, b2, ln2_g, ln2_b):
    batch, seq, _ = x_prompt.shape
    dec_batch, dec_seq, _ = x_sample.shape

    cond = jnp.concatenate(
        [c_ctx[None, :], c, jnp.zeros((COND_ROWS - 1 - dec_batch, D_MODEL), F32)], axis=0)
    mod = _modulation(cond, w_mod, b_mod)
    tables = _bias_tables(rpb)

    w_in_l = w_in[0].astype(BF16)
    post_mats_f32 = [w_attn_proj, w_conv_proj, w_o, w1, w2]
    w_kv_t = _transposed_kv_weights(w_in)
    cache_k_t = jnp.swapaxes(cache_k, -1, -2)
    cache_v_t = jnp.swapaxes(cache_v, -1, -2)
    weights = {
        "conv_w": conv_w,
        "vectors": [v.reshape(DEPTH, 1, -1) for v in (conv_b, ln1_g, ln1_b, b1, b2, ln2_g, ln2_b)],
    }

    xp = x_prompt.reshape(batch * seq, D_MODEL)
    xs = x_sample.reshape(dec_batch * dec_seq, D_MODEL)
    kv = None
    for layer in range(DEPTH):
        q, new_k_t, new_v_t, z, bg, ga, gc, *mats = _inproj(
            xp, mod, w_in_l, w_kv_t, layer, n_seq=batch, seq_len=seq, mod_row0=0, kv_prev=kv,
            cast_src=post_mats_f32)
        kv = (new_k_t, new_v_t)
        attn = _ctx_attention(q, new_k_t, new_v_t, layer)
        last = layer == DEPTH - 1
        xp, *w_in_next = _post(xp, attn, z, bg, ga, gc, mod, mats, weights, layer, seq_len=seq,
                               mod_row0=0, cast_src=() if last else (w_in,), cast_layer=layer + 1)

        q, k_t, v_t, z, bg, ga, gc = _inproj(
            xs, mod, w_in_l, w_kv_t, layer, n_seq=dec_batch, seq_len=dec_seq, mod_row0=1)
        attn = _lat_attention(q, k_t, v_t, cache_k_t, cache_v_t, tables, layer)
        (xs,) = _post(xs, attn.reshape(dec_batch * dec_seq, ATTN_WIDTH), z, bg, ga, gc, mod, mats,
                      weights, layer, seq_len=dec_seq, mod_row0=1)
        if not last:
            (w_in_l,) = w_in_next

    new_k = jnp.swapaxes(kv[0], -1, -2)
    new_v = jnp.swapaxes(kv[1], -1, -2)
    return (xp.reshape(batch, seq, D_MODEL), xs.reshape(dec_batch, dec_seq, D_MODEL), new_k, new_v)
```

```python
import functools

import jax
import jax.numpy as jnp
from jax import lax
from jax.experimental import pallas as pl
from jax.experimental.pallas import tpu as pltpu

D_MODEL = 1024
DEPTH = 4
N_HEADS = 8
HEAD_DIM = 64
ATTN_WIDTH = N_HEADS * HEAD_DIM
CONV_WIDTH = D_MODEL // 2
D_FF = 4 * D_MODEL
GRID_W = 64
WIN_R = 8
WIN_C = 16
RPB_ROWS = 2 * WIN_R - 1
RPB_COLS = 2 * WIN_C - 1
IN_COLS = 3 * ATTN_WIDTH + 3 * CONV_WIDTH + 2 * D_MODEL
SCALE = HEAD_DIM ** -0.5
ALPHA = (2.0 * DEPTH) ** 0.25
LN_EPS = 1e-5

COND_ROWS = 8
INPROJ_TILE = 512
TOKEN_TILE = 512
FF_CHUNK = 1024
POST_ROW_GROUP = 256
HALO_ROWS = 16
ROWS_PER_CHUNK = 4
VMEM_LIMIT_BYTES = 56 * 1024 * 1024

BF16 = jnp.bfloat16
F32 = jnp.float32
NT_DIMS = (((1,), (1,)), ((), ()))


def _dot(a, b):
    return jnp.dot(a, b, preferred_element_type=F32)


def _dot_nt(a, b):
    return lax.dot_general(a, b, NT_DIMS, preferred_element_type=F32)


def _resident(shape):
    zeros = (0,) * len(shape)
    return pl.BlockSpec(tuple(shape), lambda *_: zeros, pipeline_mode=pl.Buffered(1))


def _layer_resident(shape, layer):
    zeros = (0,) * len(shape)
    return pl.BlockSpec((None,) + tuple(shape), lambda *_: (layer,) + zeros,
                        pipeline_mode=pl.Buffered(1))


def _cond_row(tile, mod_row0, tiles_per_seq):
    if mod_row0 == 0:
        return 0
    return mod_row0 + tile // tiles_per_seq


def _params(n_grid_dims):
    return pltpu.CompilerParams(
        dimension_semantics=("arbitrary",) * n_grid_dims,
        vmem_limit_bytes=VMEM_LIMIT_BYTES,
    )


def _cast_rows(src_refs, dst_refs):
    for src, dst in zip(src_refs, dst_refs, strict=True):
        dst[...] = src[...].astype(BF16)


def _row_chunk_specs(stacked, layer, n_steps):
    in_specs, out_specs, out_shapes = [], [], []
    for w in stacked:
        _, n_rows, n_cols = w.shape
        chunk = n_rows // n_steps
        assert chunk * n_steps == n_rows and chunk % 16 == 0
        in_specs.append(pl.BlockSpec((None, chunk, n_cols), lambda i: (layer, i, 0)))
        out_specs.append(pl.BlockSpec((chunk, n_cols), lambda i: (i, 0)))
        out_shapes.append(jax.ShapeDtypeStruct((n_rows, n_cols), BF16))
    return in_specs, out_specs, out_shapes


MOD_COL_TILE = 1536


def _mod_kernel(cond_ref, w_ref, b_ref, o_ref):
    c = cond_ref[...]
    s = (c * jax.nn.sigmoid(c)).astype(BF16)
    o_ref[0] = _dot(s, w_ref[0].astype(BF16)) + b_ref[0]


def _modulation(cond, w_mod, b_mod):
    n_cols = 6 * D_MODEL
    return pl.pallas_call(
        _mod_kernel,
        grid=(DEPTH, n_cols // MOD_COL_TILE),
        in_specs=[
            pl.BlockSpec((COND_ROWS, D_MODEL), lambda l, j: (0, 0)),
            pl.BlockSpec((1, D_MODEL, MOD_COL_TILE), lambda l, j: (l, 0, j)),
            pl.BlockSpec((1, 1, MOD_COL_TILE), lambda l, j: (l, 0, j)),
        ],
        out_specs=pl.BlockSpec((1, COND_ROWS, MOD_COL_TILE), lambda l, j: (l, 0, j)),
        out_shape=jax.ShapeDtypeStruct((DEPTH, COND_ROWS, n_cols), F32),
        compiler_params=_params(2),
        name="modulation",
    )(cond, w_mod, b_mod.reshape(DEPTH, 1, n_cols))


UNION_ROWS = WIN_R + ROWS_PER_CHUNK
N_CHUNK_KINDS = 3
assert ROWS_PER_CHUNK == WIN_R // 2


def _bias_table_kernel(rpb_ref, o_ref):
    l = pl.program_id(0)
    h = pl.program_id(1)
    base = (l * N_HEADS + h) * (RPB_ROWS * RPB_COLS)
    shape = (GRID_W, 2 * GRID_W)
    q = lax.broadcasted_iota(jnp.int32, shape, 0)
    lane = lax.broadcasted_iota(jnp.int32, shape, 1)
    kc = lane & (GRID_W - 1)
    col_start = jnp.clip(q - WIN_C // 2, 0, GRID_W - WIN_C)
    valid = (kc >= col_start) & (kc < col_start + WIN_C)
    neg_inf = jnp.full(shape, -jnp.inf, F32)
    sub = lax.broadcasted_iota(jnp.int32, (8, 2 * GRID_W), 0)
    lane8 = lax.broadcasted_iota(jnp.int32, (8, 2 * GRID_W), 1)
    pad = GRID_W // 2
    t = lane8 - pad - sub + (WIN_C - 1)
    per_row = []
    for j in range(RPB_ROWS):
        tile = jnp.full((8, 2 * GRID_W), -jnp.inf, F32)
        for m in range(RPB_COLS):
            tile = jnp.where(t == m, rpb_ref[base + j * RPB_COLS + m], tile)
        groups = []
        for a in range(GRID_W // 8):
            low = pltpu.roll(tile, (8 * a - pad) % (2 * GRID_W), 1)
            high = pltpu.roll(tile, 8 * a - pad + GRID_W, 1)
            groups.append(jnp.where(lane8 < GRID_W, low, high))
        per_row.append(jnp.where(valid, jnp.concatenate(groups, axis=0), neg_inf))
    chunk_kinds = (
        [(0, g) for g in range(ROWS_PER_CHUNK)],
        [(g, WIN_R // 2) for g in range(ROWS_PER_CHUNK)],
        [(ROWS_PER_CHUNK, WIN_R // 2 + g) for g in range(ROWS_PER_CHUNK)],
    )
    for kind, rows in enumerate(chunk_kinds):
        for g, (offset, d) in enumerate(rows):
            for pair in range(UNION_ROWS // 2):
                halves = []
                for j in (2 * pair, 2 * pair + 1):
                    i = j - offset
                    halves.append(per_row[i + WIN_R - 1 - d] if 0 <= i < WIN_R else neg_inf)
                o_ref[0, 0, kind, g * GRID_W:(g + 1) * GRID_W, pair * 128:(pair + 1) * 128] = (
                    jnp.where(lane < GRID_W, halves[0], halves[1]))


def _bias_tables(rpb):
    blk = (1, 1, N_CHUNK_KINDS, ROWS_PER_CHUNK * GRID_W, UNION_ROWS * GRID_W)
    return pl.pallas_call(
        _bias_table_kernel,
        grid=(DEPTH, N_HEADS),
        in_specs=[pl.BlockSpec(memory_space=pltpu.SMEM)],
        out_specs=pl.BlockSpec(blk, lambda l, h: (l, h, 0, 0, 0)),
        out_shape=jax.ShapeDtypeStruct((DEPTH, N_HEADS) + blk[2:], F32),
        compiler_params=_params(2),
        name="bias_tables",
    )(rpb.reshape(-1))


def _transpose_cast_kernel(w_ref, o_ref):
    o_ref[...] = w_ref[...].T.astype(BF16)


def _transposed_kv_weights(w_in):
    return pl.pallas_call(
        _transpose_cast_kernel,
        grid=(DEPTH, 2),
        in_specs=[pl.BlockSpec((None, D_MODEL, ATTN_WIDTH), lambda l, j: (l, 0, 1 + j))],
        out_specs=pl.BlockSpec((None, ATTN_WIDTH, D_MODEL), lambda l, j: (l, j, 0)),
        out_shape=jax.ShapeDtypeStruct((DEPTH, 2 * ATTN_WIDTH, D_MODEL), BF16),
        compiler_params=_params(2),
        name="kv_weight_transpose",
    )(w_in)


def _inproj_kernel(x_ref, mod_ref, w_ref, wkv_t_ref, *rest, mod_row0, tiles_per_seq, seqs_per_tile,
                   n_prev, n_cast, fuse_attn):
    cast_src = rest[n_prev:n_prev + n_cast]
    outs = rest[n_prev + n_cast:]
    first_ref, kt_ref, vt_ref, z_ref, bg_ref, ga_ref, gc_ref = outs[:7]
    _cast_rows(cast_src, outs[7:])
    i = pl.program_id(0)
    row = _cond_row(i, mod_row0, tiles_per_seq)
    sh1 = mod_ref[0, pl.ds(row, 1), 0:D_MODEL]
    sc1 = mod_ref[0, pl.ds(row, 1), D_MODEL:2 * D_MODEL]
    h = (x_ref[...] * (1.0 + sc1) + sh1).astype(BF16)

    def proj(lo, hi):
        return _dot(h, w_ref[:, lo:hi])

    a = ATTN_WIDTH
    c0 = 3 * ATTN_WIDTH
    g0 = c0 + 3 * CONV_WIDTH
    rows = INPROJ_TILE // seqs_per_tile
    head = lambda hd: slice(hd * HEAD_DIM, (hd + 1) * HEAD_DIM)
    seq = lambda s: slice(s * rows, (s + 1) * rows)

    q = proj(0, a) * SCALE
    kv_t = []
    for lo, ref in ((0, kt_ref), (a, vt_ref)):
        y_t = _dot_nt(wkv_t_ref[lo:lo + a, :], h)
        kv_t.append(y_t)
        for s in range(seqs_per_tile):
            for hd in range(N_HEADS):
                ref[s, 0, hd] = y_t[head(hd), seq(s)].astype(ref.dtype)

    def gate_a():
        ga_ref[...] = proj(g0, g0 + D_MODEL).astype(BF16)

    def gate_c():
        gc_ref[...] = proj(g0 + D_MODEL, g0 + 2 * D_MODEL).astype(BF16)

    def conv_input():
        u = proj(c0, c0 + CONV_WIDTH)
        cg = proj(c0 + 2 * CONV_WIDTH, c0 + 3 * CONV_WIDTH)
        z_ref[...] = (cg * u).astype(BF16)

    def conv_gate():
        bg_ref[...] = proj(c0 + CONV_WIDTH, c0 + 2 * CONV_WIDTH).astype(BF16)

    remaining = [gate_a, gate_c, conv_input, conv_gate]
    if not fuse_attn:
        for s in range(seqs_per_tile):
            for hd in range(N_HEADS):
                first_ref[s, hd] = q[seq(s), head(hd)].astype(BF16)
        for emit in remaining:
            emit()
        return

    items = [(s, hd) for s in range(seqs_per_tile) for hd in range(N_HEADS)]
    every = len(items) // len(remaining)
    done = {}

    def score(n):
        s, hd = items[n]
        sc = _dot(q[seq(s), head(hd)].astype(BF16), kv_t[0][head(hd), seq(s)].astype(BF16))
        if n % every == 1:
            remaining.pop(0)()
        return sc

    def normalise(n, sc):
        p = jnp.exp(sc - jnp.max(sc, axis=-1, keepdims=True))
        return p.astype(BF16), jnp.sum(p, axis=-1, keepdims=True)

    def finish(n, prob):
        s, hd = items[n]
        p, denom = prob
        done[hd] = _dot_nt(p, kv_t[1][head(hd), seq(s)].astype(BF16)) / denom
        if hd % 2 == 1:
            first_ref[seq(s), (hd - 1) * HEAD_DIM:(hd + 1) * HEAD_DIM] = jnp.concatenate(
                [done.pop(hd - 1), done.pop(hd)], axis=-1).astype(BF16)

    _software_pipeline(len(items), score, normalise, finish)
    assert not remaining


def _inproj(x, mod, w_in_l, w_kv_t, layer, *, n_seq, seq_len, mod_row0, kv_prev=None, cast_src=()):
    n_tok = n_seq * seq_len
    n_tiles = n_tok // INPROJ_TILE
    is_ctx = mod_row0 == 0
    if seq_len >= INPROJ_TILE:
        tps, spt, rows = seq_len // INPROJ_TILE, 1, INPROJ_TILE
        head_map = lambda i: (i // tps, 0, i % tps, 0)
        kv_map = lambda i: (i // tps, layer if is_ctx else 0, 0, 0, i % tps)
    else:
        tps, spt, rows = 1, INPROJ_TILE // seq_len, seq_len
        head_map = lambda i: (i, 0, 0, 0)
        kv_map = lambda i: (i, layer if is_ctx else 0, 0, 0, 0)
    head_blk = (spt, N_HEADS, rows, HEAD_DIM)
    fuse_attn = is_ctx and seq_len <= INPROJ_TILE
    kv_blk = (spt, 1, N_HEADS, HEAD_DIM, rows)
    if is_ctx:
        kv_shape = jax.ShapeDtypeStruct((n_seq, DEPTH, N_HEADS, HEAD_DIM, seq_len), F32)
    else:
        kv_shape = jax.ShapeDtypeStruct((n_seq, 1, N_HEADS, HEAD_DIM, seq_len), BF16)
    tok = lambda width: pl.BlockSpec((INPROJ_TILE, width), lambda i: (i, 0))
    tok_shape = lambda width: jax.ShapeDtypeStruct((n_tok, width), BF16)
    in_specs = [
        tok(D_MODEL),
        pl.BlockSpec((1, COND_ROWS, 6 * D_MODEL), lambda i: (layer, 0, 0)),
        _resident((D_MODEL, IN_COLS)),
        _layer_resident((2 * ATTN_WIDTH, D_MODEL), layer),
    ]
    args = [x, mod, w_in_l, w_kv_t]
    aliases = {}
    n_prev = 0
    if kv_prev is not None:
        in_specs += [pl.BlockSpec(memory_space=pl.ANY)] * 2
        args += list(kv_prev)
        aliases = {4: 1, 5: 2}
        n_prev = 2
    cast_in, cast_out, cast_shapes = _row_chunk_specs(cast_src, layer, n_tiles)
    args += list(cast_src)
    return pl.pallas_call(
        functools.partial(_inproj_kernel, mod_row0=mod_row0, tiles_per_seq=tps, seqs_per_tile=spt,
                          n_prev=n_prev, n_cast=len(cast_src), fuse_attn=fuse_attn),
        grid=(n_tiles,),
        in_specs=in_specs + cast_in,
        out_specs=[
            tok(ATTN_WIDTH) if fuse_attn else pl.BlockSpec(head_blk, head_map),
            pl.BlockSpec(kv_blk, kv_map),
            pl.BlockSpec(kv_blk, kv_map),
            tok(CONV_WIDTH), tok(CONV_WIDTH), tok(D_MODEL), tok(D_MODEL),
        ] + cast_out,
        out_shape=[
            (tok_shape(ATTN_WIDTH) if fuse_attn
             else jax.ShapeDtypeStruct((n_seq, N_HEADS, seq_len, HEAD_DIM), BF16)),
            kv_shape, kv_shape,
            tok_shape(CONV_WIDTH), tok_shape(CONV_WIDTH), tok_shape(D_MODEL), tok_shape(D_MODEL),
        ] + cast_shapes,
        input_output_aliases=aliases,
        compiler_params=_params(1),
        name="inproj_ctx" if is_ctx else "inproj_lat",
    )(*args)


def _software_pipeline(n_items, score, normalise, finish):
    scores = {0: score(0)}
    probs = {}
    for i in range(n_items + 1):
        if i + 1 < n_items:
            scores[i + 1] = score(i + 1)
        if i < n_items:
            probs[i] = normalise(i, scores.pop(i))
        if i >= 1:
            finish(i - 1, probs.pop(i - 1))


HEADS_PER_STEP = 2


def _lat_attn_kernel(q_ref, kt_ref, vt_ref, kct_ref, vct_ref, tab_ref, o_ref, *, n_rows):
    n_chunks = n_rows // ROWS_PER_CHUNK
    chunk_q = ROWS_PER_CHUNK * GRID_W
    union = UNION_ROWS * GRID_W
    items = [(c, hh) for c in range(n_chunks) for hh in range(HEADS_PER_STEP)]
    done = {}

    def key_cols(c):
        first_key_row = min(max(c * ROWS_PER_CHUNK - WIN_R // 2, 0), n_rows - UNION_ROWS)
        return slice(first_key_row * GRID_W, first_key_row * GRID_W + union)

    def score(i):
        c, hh = items[i]
        kind = 0 if c == 0 else (2 if c == n_chunks - 1 else 1)
        qc = q_ref[0, hh, c * chunk_q:(c + 1) * chunk_q, :]
        s_band = _dot(qc, kt_ref[0, 0, hh, :, key_cols(c)]) + tab_ref[0, hh, kind]
        s_ctx = _dot(qc, kct_ref[0, 0, hh].astype(BF16))
        return s_band, s_ctx

    def normalise(i, scores):
        s_band, s_ctx = scores
        m = jnp.maximum(jnp.max(s_band, axis=-1, keepdims=True),
                        jnp.max(s_ctx, axis=-1, keepdims=True))
        p_band = jnp.exp(s_band - m)
        p_ctx = jnp.exp(s_ctx - m)
        denom = jnp.sum(p_band, axis=-1, keepdims=True) + jnp.sum(p_ctx, axis=-1, keepdims=True)
        return p_band.astype(BF16), p_ctx.astype(BF16), denom

    def finish(i, prob):
        c, hh = items[i]
        p_band, p_ctx, denom = prob
        o = (_dot_nt(p_band, vt_ref[0, 0, hh, :, key_cols(c)])
             + _dot_nt(p_ctx, vct_ref[0, 0, hh].astype(BF16)))
        done[hh] = o / denom
        if hh == HEADS_PER_STEP - 1:
            o_ref[0, c * chunk_q:(c + 1) * chunk_q, :] = jnp.concatenate(
                [done.pop(h) for h in range(HEADS_PER_STEP)], axis=-1).astype(BF16)

    _software_pipeline(len(items), score, normalise, finish)


def _lat_attention(q, k_t, v_t, cache_k_t, cache_v_t, tables, layer):
    n_seq, _, seq_len, _ = q.shape
    past_len = cache_k_t.shape[4]
    n_rows = seq_len // GRID_W
    assert n_rows % ROWS_PER_CHUNK == 0 and n_rows >= 2 * UNION_ROWS
    hps = HEADS_PER_STEP
    return pl.pallas_call(
        functools.partial(_lat_attn_kernel, n_rows=n_rows),
        grid=(N_HEADS // hps, n_seq),
        in_specs=[
            pl.BlockSpec((1, hps, seq_len, HEAD_DIM), lambda p, b: (b, p, 0, 0)),
            pl.BlockSpec((1, 1, hps, HEAD_DIM, seq_len), lambda p, b: (b, 0, p, 0, 0)),
            pl.BlockSpec((1, 1, hps, HEAD_DIM, seq_len), lambda p, b: (b, 0, p, 0, 0)),
            pl.BlockSpec((1, 1, hps, HEAD_DIM, past_len), lambda p, b: (b, layer, p, 0, 0)),
            pl.BlockSpec((1, 1, hps, HEAD_DIM, past_len), lambda p, b: (b, layer, p, 0, 0)),
            pl.BlockSpec((1, hps) + tables.shape[2:], lambda p, b: (layer, p, 0, 0, 0)),
        ],
        out_specs=pl.BlockSpec((1, seq_len, hps * HEAD_DIM), lambda p, b: (b, 0, p)),
        out_shape=jax.ShapeDtypeStruct((n_seq, seq_len, ATTN_WIDTH), BF16),
        compiler_params=_params(2),
        name="attn_lat",
    )(q, k_t, v_t, cache_k_t, cache_v_t, tables)


def _layer_norm(y, g, b):
    mu = jnp.mean(y, axis=-1, keepdims=True)
    yc = y - mu
    var = jnp.mean(yc * yc, axis=-1, keepdims=True)
    return yc * lax.rsqrt(var + LN_EPS) * g + b


def _post_kernel(x_ref, attn_ref, z_ref, zprev_ref, znext_ref, bg_ref, ga_ref, gc_ref, mod_ref,
                 wap_ref, wcp_ref, wo_ref, w1_ref, w2_ref, cw_ref, cb_ref,
                 ln1g_ref, ln1b_ref, b1_ref, b2_ref, ln2g_ref, ln2b_ref, *rest,
                 mod_row0, tiles_per_seq, seq_len, n_cast):
    o_ref = rest[n_cast]
    _cast_rows(rest[:n_cast], rest[n_cast + 1:])
    i = pl.program_id(0)
    row = _cond_row(i, mod_row0, tiles_per_seq)
    d = D_MODEL
    g1 = mod_ref[0, pl.ds(row, 1), 2 * d:3 * d]
    sh2 = mod_ref[0, pl.ds(row, 1), 3 * d:4 * d]
    sc2 = mod_ref[0, pl.ds(row, 1), 4 * d:5 * d]
    g2 = mod_ref[0, pl.ds(row, 1), 5 * d:6 * d]

    z = z_ref[...].astype(F32)
    prev_row = zprev_ref[...].astype(F32)[HALO_ROWS - 1:HALO_ROWS, :]
    next_row = znext_ref[...].astype(F32)[0:1, :]
    pos = lax.broadcasted_iota(jnp.int32, z.shape, 0)
    seq_pos = (i * TOKEN_TILE + pos) & (seq_len - 1)
    z_before = jnp.where(pos == 0, prev_row, pltpu.roll(z, 1, 0))
    z_before = jnp.where(seq_pos == 0, 0.0, z_before)
    z_after = jnp.where(pos == TOKEN_TILE - 1, next_row, pltpu.roll(z, TOKEN_TILE - 1, 0))
    z_after = jnp.where(seq_pos == seq_len - 1, 0.0, z_after)
    conv = z_before * cw_ref[0:1, :] + z * cw_ref[1:2, :] + z_after * cw_ref[2:3, :] + cb_ref[...]
    conv_in = (bg_ref[...].astype(F32) * conv).astype(BF16)

    def merge(rs):
        attn_p = _dot(attn_ref[rs, :], wap_ref[...])
        conv_p = _dot(conv_in[rs, :], wcp_ref[...])
        return (jax.nn.sigmoid(ga_ref[rs, :].astype(F32)) * attn_p
                + jax.nn.sigmoid(gc_ref[rs, :].astype(F32)) * conv_p).astype(BF16)

    def mix_norm(merged, rs):
        mix = _dot(merged, wo_ref[...])
        x1 = _layer_norm(ALPHA * x_ref[rs, :] + g1 * mix, ln1g_ref[...], ln1b_ref[...])
        return x1, (x1 * (1.0 + sc2) + sh2).astype(BF16)

    def mlp_chunk(h2, c):
        cols = slice(c * FF_CHUNK, (c + 1) * FF_CHUNK)
        hidden = jnp.maximum(_dot(h2, w1_ref[:, cols]) + b1_ref[:, cols], 0.0)
        return _dot((hidden * hidden).astype(BF16), w2_ref[cols, :])

    def finish(x1, f, rs):
        o_ref[rs, :] = _layer_norm(ALPHA * x1 + g2 * f, ln2g_ref[...], ln2b_ref[...])

    groups = [slice(r0, r0 + POST_ROW_GROUP) for r0 in range(0, TOKEN_TILE, POST_ROW_GROUP)]
    n_chunks = D_FF // FF_CHUNK
    x1, h2 = mix_norm(merge(groups[0]), groups[0])
    f = b2_ref[...] + mlp_chunk(h2, 0)
    for n, rs in enumerate(groups):
        nxt = groups[n + 1] if n + 1 < len(groups) else None
        if nxt is not None:
            merged_next = merge(nxt)
        f = f + mlp_chunk(h2, 1)
        if nxt is not None:
            x1_next, h2_next = mix_norm(merged_next, nxt)
        for c in range(2, n_chunks):
            f = f + mlp_chunk(h2, c)
        if nxt is not None:
            f_next = b2_ref[...] + mlp_chunk(h2_next, 0)
        finish(x1, f, rs)
        if nxt is not None:
            x1, h2, f = x1_next, h2_next, f_next


def _post(x, attn, z, bg, ga, gc, mod, mats, weights, layer, *, seq_len, mod_row0, cast_src=(),
          cast_layer=None):
    n_tok = x.shape[0]
    assert seq_len & (seq_len - 1) == 0
    tps = max(seq_len // TOKEN_TILE, 1)
    n_tiles = n_tok // TOKEN_TILE
    halo_per_tile = TOKEN_TILE // HALO_ROWS
    n_halo_blocks = n_tok // HALO_ROWS
    tok = lambda width: pl.BlockSpec((TOKEN_TILE, width), lambda i: (i, 0))
    in_specs = [
        tok(D_MODEL), tok(ATTN_WIDTH), tok(CONV_WIDTH),
        pl.BlockSpec((HALO_ROWS, CONV_WIDTH), lambda i: (jnp.maximum(i * halo_per_tile - 1, 0), 0)),
        pl.BlockSpec((HALO_ROWS, CONV_WIDTH),
                     lambda i: (jnp.minimum((i + 1) * halo_per_tile, n_halo_blocks - 1), 0)),
        tok(CONV_WIDTH), tok(D_MODEL), tok(D_MODEL),
        pl.BlockSpec((1, COND_ROWS, 6 * D_MODEL), lambda i: (layer, 0, 0)),
    ]
    per_layer = [weights["conv_w"]] + weights["vectors"]
    in_specs += [_resident(m.shape) for m in mats]
    in_specs += [_layer_resident(w.shape[1:], layer) for w in per_layer]
    cast_in, cast_out, cast_shapes = _row_chunk_specs(cast_src, cast_layer, n_tiles)
    return pl.pallas_call(
        functools.partial(_post_kernel, mod_row0=mod_row0, tiles_per_seq=tps, seq_len=seq_len,
                          n_cast=len(cast_src)),
        grid=(n_tiles,),
        in_specs=in_specs + cast_in,
        out_specs=[tok(D_MODEL)] + cast_out,
        out_shape=[jax.ShapeDtypeStruct((n_tok, D_MODEL), F32)] + cast_shapes,
        compiler_params=_params(1),
        name="post_ctx" if mod_row0 == 0 else "post_lat",
    )(x, attn, z, z, z, bg, ga, gc, mod, *mats, *per_layer, *cast_src)


def kernel(x_prompt, x_sample, cache_k, cache_v, c, c_ctx, w_mod, b_mod, w_in, rpb, conv_w, conv_b,
           w_attn_proj, w_conv_proj, w_o, ln1_g, ln1_b, w1, b1, w2, b2, ln2_g, ln2_b):
    batch, seq, _ = x_prompt.shape
    dec_batch, dec_seq, _ = x_sample.shape

    cond = jnp.concatenate(
        [c_ctx[None, :], c, jnp.zeros((COND_ROWS - 1 - dec_batch, D_MODEL), F32)], axis=0)
    mod = _modulation(cond, w_mod, b_mod)
    tables = _bias_tables(rpb)

    w_in_l = w_in[0].astype(BF16)
    post_mats_f32 = [w_attn_proj, w_conv_proj, w_o, w1, w2]
    w_kv_t = _transposed_kv_weights(w_in)
    cache_k_t = jnp.swapaxes(cache_k, -1, -2)
    cache_v_t = jnp.swapaxes(cache_v, -1, -2)
    weights = {
        "conv_w": conv_w,
        "vectors": [v.reshape(DEPTH, 1, -1) for v in (conv_b, ln1_g, ln1_b, b1, b2, ln2_g, ln2_b)],
    }

    xp = x_prompt.reshape(batch * seq, D_MODEL)
    xs = x_sample.reshape(dec_batch * dec_seq, D_MODEL)
    kv = None
    for layer in range(DEPTH):
        attn, new_k_t, new_v_t, z, bg, ga, gc, *mats = _inproj(
            xp, mod, w_in_l, w_kv_t, layer, n_seq=batch, seq_len=seq, mod_row0=0, kv_prev=kv,
            cast_src=post_mats_f32)
        kv = (new_k_t, new_v_t)
        last = layer == DEPTH - 1
        xp, *w_in_next = _post(xp, attn, z, bg, ga, gc, mod, mats, weights, layer, seq_len=seq,
                               mod_row0=0, cast_src=() if last else (w_in,), cast_layer=layer + 1)

        q, k_t, v_t, z, bg, ga, gc = _inproj(
            xs, mod, w_in_l, w_kv_t, layer, n_seq=dec_batch, seq_len=dec_seq, mod_row0=1)
        attn = _lat_attention(q, k_t, v_t, cache_k_t, cache_v_t, tables, layer)
        (xs,) = _post(xs, attn.reshape(dec_batch * dec_seq, ATTN_WIDTH), z, bg, ga, gc, mod, mats,
                      weights, layer, seq_len=dec_seq, mod_row0=1)
        if not last:
            (w_in_l,) = w_in_next

    new_k = jnp.swapaxes(kv[0], -1, -2)
    new_v = jnp.swapaxes(kv[1], -1, -2)
    return (xp.reshape(batch, seq, D_MODEL), xs.reshape(dec_batch, dec_seq, D_MODEL), new_k, new_v)
```

```python
import functools

import jax
import jax.numpy as jnp
from jax import lax
from jax.experimental import pallas as pl
from jax.experimental.pallas import tpu as pltpu

D_MODEL = 1024
DEPTH = 4
N_HEADS = 8
HEAD_DIM = 64
ATTN_WIDTH = N_HEADS * HEAD_DIM
CONV_WIDTH = D_MODEL // 2
D_FF = 4 * D_MODEL
GRID_W = 64
WIN_R = 8
WIN_C = 16
RPB_ROWS = 2 * WIN_R - 1
RPB_COLS = 2 * WIN_C - 1
IN_COLS = 3 * ATTN_WIDTH + 3 * CONV_WIDTH + 2 * D_MODEL
SCALE = HEAD_DIM ** -0.5
ALPHA = (2.0 * DEPTH) ** 0.25
LN_EPS = 1e-5

COND_ROWS = 8
INPROJ_TILE = 512
TOKEN_TILE = 512
FF_CHUNK = 1024
POST_ROW_GROUP = 256
HALO_ROWS = 16
ROWS_PER_CHUNK = 4
VMEM_LIMIT_BYTES = 56 * 1024 * 1024

BF16 = jnp.bfloat16
F32 = jnp.float32
NT_DIMS = (((1,), (1,)), ((), ()))


def _dot(a, b):
    return jnp.dot(a, b, preferred_element_type=F32)


def _dot_nt(a, b):
    return lax.dot_general(a, b, NT_DIMS, preferred_element_type=F32)


def _resident(shape):
    zeros = (0,) * len(shape)
    return pl.BlockSpec(tuple(shape), lambda *_: zeros, pipeline_mode=pl.Buffered(1))


def _layer_resident(shape, layer):
    zeros = (0,) * len(shape)
    return pl.BlockSpec((None,) + tuple(shape), lambda *_: (layer,) + zeros,
                        pipeline_mode=pl.Buffered(1))


def _cond_row(tile, mod_row0, tiles_per_seq):
    if mod_row0 == 0:
        return 0
    return mod_row0 + tile // tiles_per_seq


def _params(n_grid_dims):
    return pltpu.CompilerParams(
        dimension_semantics=("arbitrary",) * n_grid_dims,
        vmem_limit_bytes=VMEM_LIMIT_BYTES,
    )


def _cast_rows(src_refs, dst_refs):
    for src, dst in zip(src_refs, dst_refs, strict=True):
        dst[...] = src[...].astype(BF16)


def _row_chunk_specs(stacked, layer, n_steps):
    in_specs, out_specs, out_shapes = [], [], []
    for w in stacked:
        _, n_rows, n_cols = w.shape
        chunk = n_rows // n_steps
        assert chunk * n_steps == n_rows and chunk % 16 == 0
        in_specs.append(pl.BlockSpec((None, chunk, n_cols), lambda i: (layer, i, 0)))
        out_specs.append(pl.BlockSpec((chunk, n_cols), lambda i: (i, 0)))
        out_shapes.append(jax.ShapeDtypeStruct((n_rows, n_cols), BF16))
    return in_specs, out_specs, out_shapes


MOD_COL_TILE = 1536


def _mod_kernel(cond_ref, w_ref, b_ref, o_ref):
    c = cond_ref[...]
    s = (c * jax.nn.sigmoid(c)).astype(BF16)
    o_ref[0] = _dot(s, w_ref[0].astype(BF16)) + b_ref[0]


def _modulation(cond, w_mod, b_mod):
    n_cols = 6 * D_MODEL
    return pl.pallas_call(
        _mod_kernel,
        grid=(DEPTH, n_cols // MOD_COL_TILE),
        in_specs=[
            pl.BlockSpec((COND_ROWS, D_MODEL), lambda l, j: (0, 0)),
            pl.BlockSpec((1, D_MODEL, MOD_COL_TILE), lambda l, j: (l, 0, j)),
            pl.BlockSpec((1, 1, MOD_COL_TILE), lambda l, j: (l, 0, j)),
        ],
        out_specs=pl.BlockSpec((1, COND_ROWS, MOD_COL_TILE), lambda l, j: (l, 0, j)),
        out_shape=jax.ShapeDtypeStruct((DEPTH, COND_ROWS, n_cols), F32),
        compiler_params=_params(2),
        name="modulation",
    )(cond, w_mod, b_mod.reshape(DEPTH, 1, n_cols))


UNION_ROWS = WIN_R + ROWS_PER_CHUNK
N_CHUNK_KINDS = 3
assert ROWS_PER_CHUNK == WIN_R // 2


def _write_score_tables(rpb_ref, base, o_ref):
    shape = (GRID_W, 2 * GRID_W)
    q = lax.broadcasted_iota(jnp.int32, shape, 0)
    lane = lax.broadcasted_iota(jnp.int32, shape, 1)
    kc = lane & (GRID_W - 1)
    col_start = jnp.clip(q - WIN_C // 2, 0, GRID_W - WIN_C)
    valid = (kc >= col_start) & (kc < col_start + WIN_C)
    neg_inf = jnp.full(shape, -jnp.inf, F32)
    sub = lax.broadcasted_iota(jnp.int32, (8, 2 * GRID_W), 0)
    lane8 = lax.broadcasted_iota(jnp.int32, (8, 2 * GRID_W), 1)
    pad = GRID_W // 2
    t = lane8 - pad - sub + (WIN_C - 1)
    per_row = []
    for j in range(RPB_ROWS):
        tile = jnp.full((8, 2 * GRID_W), -jnp.inf, F32)
        for m in range(RPB_COLS):
            tile = jnp.where(t == m, rpb_ref[base + j * RPB_COLS + m], tile)
        groups = []
        for a in range(GRID_W // 8):
            low = pltpu.roll(tile, (8 * a - pad) % (2 * GRID_W), 1)
            high = pltpu.roll(tile, 8 * a - pad + GRID_W, 1)
            groups.append(jnp.where(lane8 < GRID_W, low, high))
        per_row.append(jnp.where(valid, jnp.concatenate(groups, axis=0), neg_inf))
    chunk_kinds = (
        [(0, g) for g in range(ROWS_PER_CHUNK)],
        [(g, WIN_R // 2) for g in range(ROWS_PER_CHUNK)],
        [(ROWS_PER_CHUNK, WIN_R // 2 + g) for g in range(ROWS_PER_CHUNK)],
    )
    for kind, rows in enumerate(chunk_kinds):
        for g, (offset, d) in enumerate(rows):
            for pair in range(UNION_ROWS // 2):
                halves = []
                for j in (2 * pair, 2 * pair + 1):
                    i = j - offset
                    halves.append(per_row[i + WIN_R - 1 - d] if 0 <= i < WIN_R else neg_inf)
                o_ref[0, kind, g * GRID_W:(g + 1) * GRID_W, pair * 128:(pair + 1) * 128] = (
                    jnp.where(lane < GRID_W, halves[0], halves[1]))


TABLE_BLOCK = (1, N_CHUNK_KINDS, ROWS_PER_CHUNK * GRID_W, UNION_ROWS * GRID_W)


def _transpose_cast_kernel(w_ref, o_ref):
    o_ref[...] = w_ref[...].T.astype(BF16)


def _transposed_kv_weights(w_in):
    return pl.pallas_call(
        _transpose_cast_kernel,
        grid=(DEPTH, 2),
        in_specs=[pl.BlockSpec((None, D_MODEL, ATTN_WIDTH), lambda l, j: (l, 0, 1 + j))],
        out_specs=pl.BlockSpec((None, ATTN_WIDTH, D_MODEL), lambda l, j: (l, j, 0)),
        out_shape=jax.ShapeDtypeStruct((DEPTH, 2 * ATTN_WIDTH, D_MODEL), BF16),
        compiler_params=_params(2),
        name="kv_weight_transpose",
    )(w_in)


def _inproj_kernel(x_ref, mod_ref, w_ref, wkv_t_ref, *rest, mod_row0, tiles_per_seq, seqs_per_tile,
                   n_prev, n_cast, fuse_attn, table_layer):
    cast_src = rest[n_prev:n_prev + n_cast]
    n_in = n_prev + n_cast + (table_layer is not None)
    outs = rest[n_in:]
    first_ref, kt_ref, vt_ref, z_ref, bg_ref, ga_ref, gc_ref = outs[:7]
    _cast_rows(cast_src, outs[7:7 + n_cast])
    i = pl.program_id(0)
    row = _cond_row(i, mod_row0, tiles_per_seq)
    sh1 = mod_ref[0, pl.ds(row, 1), 0:D_MODEL]
    sc1 = mod_ref[0, pl.ds(row, 1), D_MODEL:2 * D_MODEL]
    h = (x_ref[...] * (1.0 + sc1) + sh1).astype(BF16)

    def proj(lo, hi):
        return _dot(h, w_ref[:, lo:hi])

    a = ATTN_WIDTH
    c0 = 3 * ATTN_WIDTH
    g0 = c0 + 3 * CONV_WIDTH
    rows = INPROJ_TILE // seqs_per_tile
    head = lambda hd: slice(hd * HEAD_DIM, (hd + 1) * HEAD_DIM)
    seq = lambda s: slice(s * rows, (s + 1) * rows)

    q = proj(0, a) * SCALE
    kv_t = []
    for lo, ref in ((0, kt_ref), (a, vt_ref)):
        y_t = _dot_nt(wkv_t_ref[lo:lo + a, :], h)
        kv_t.append(y_t)
        for s in range(seqs_per_tile):
            for hd in range(N_HEADS):
                ref[s, 0, hd] = y_t[head(hd), seq(s)].astype(ref.dtype)

    def gate_a():
        ga_ref[...] = proj(g0, g0 + D_MODEL).astype(BF16)

    def gate_c():
        gc_ref[...] = proj(g0 + D_MODEL, g0 + 2 * D_MODEL).astype(BF16)

    def conv_input():
        u = proj(c0, c0 + CONV_WIDTH)
        cg = proj(c0 + 2 * CONV_WIDTH, c0 + 3 * CONV_WIDTH)
        z_ref[...] = (cg * u).astype(BF16)

    def conv_gate():
        bg_ref[...] = proj(c0 + CONV_WIDTH, c0 + 2 * CONV_WIDTH).astype(BF16)

    def score_tables():
        base = (table_layer * N_HEADS + i) * (RPB_ROWS * RPB_COLS)
        _write_score_tables(rest[n_in - 1], base, outs[-1])

    remaining = [gate_a, gate_c, conv_input, conv_gate]
    if table_layer is not None:
        remaining.insert(1, score_tables)
    if not fuse_attn:
        for s in range(seqs_per_tile):
            for hd in range(N_HEADS):
                first_ref[s, hd] = q[seq(s), head(hd)].astype(BF16)
        for emit in remaining:
            emit()
        return

    items = [(s, hd) for s in range(seqs_per_tile) for hd in range(N_HEADS)]
    every = len(items) // len(remaining)
    done = {}

    def score(n):
        s, hd = items[n]
        sc = _dot(q[seq(s), head(hd)].astype(BF16), kv_t[0][head(hd), seq(s)].astype(BF16))
        if n % every == 1:
            remaining.pop(0)()
        return sc

    def normalise(n, sc):
        p = jnp.exp(sc - jnp.max(sc, axis=-1, keepdims=True))
        return p.astype(BF16), jnp.sum(p, axis=-1, keepdims=True)

    def finish(n, prob):
        s, hd = items[n]
        p, denom = prob
        done[hd] = _dot_nt(p, kv_t[1][head(hd), seq(s)].astype(BF16)) / denom
        if hd % 2 == 1:
            first_ref[seq(s), (hd - 1) * HEAD_DIM:(hd + 1) * HEAD_DIM] = jnp.concatenate(
                [done.pop(hd - 1), done.pop(hd)], axis=-1).astype(BF16)

    _software_pipeline(len(items), score, normalise, finish)
    assert not remaining


def _inproj(x, mod, w_in_l, w_kv_t, layer, *, n_seq, seq_len, mod_row0, kv_prev=None, cast_src=(),
            rpb_flat=None):
    n_tok = n_seq * seq_len
    n_tiles = n_tok // INPROJ_TILE
    is_ctx = mod_row0 == 0
    if seq_len >= INPROJ_TILE:
        tps, spt, rows = seq_len // INPROJ_TILE, 1, INPROJ_TILE
        head_map = lambda i: (i // tps, 0, i % tps, 0)
        kv_map = lambda i: (i // tps, layer if is_ctx else 0, 0, 0, i % tps)
    else:
        tps, spt, rows = 1, INPROJ_TILE // seq_len, seq_len
        head_map = lambda i: (i, 0, 0, 0)
        kv_map = lambda i: (i, layer if is_ctx else 0, 0, 0, 0)
    head_blk = (spt, N_HEADS, rows, HEAD_DIM)
    fuse_attn = is_ctx and seq_len <= INPROJ_TILE
    kv_blk = (spt, 1, N_HEADS, HEAD_DIM, rows)
    if is_ctx:
        kv_shape = jax.ShapeDtypeStruct((n_seq, DEPTH, N_HEADS, HEAD_DIM, seq_len), F32)
    else:
        kv_shape = jax.ShapeDtypeStruct((n_seq, 1, N_HEADS, HEAD_DIM, seq_len), BF16)
    tok = lambda width: pl.BlockSpec((INPROJ_TILE, width), lambda i: (i, 0))
    tok_shape = lambda width: jax.ShapeDtypeStruct((n_tok, width), BF16)
    in_specs = [
        tok(D_MODEL),
        pl.BlockSpec((1, COND_ROWS, 6 * D_MODEL), lambda i: (layer, 0, 0)),
        _resident((D_MODEL, IN_COLS)),
        _layer_resident((2 * ATTN_WIDTH, D_MODEL), layer),
    ]
    args = [x, mod, w_in_l, w_kv_t]
    aliases = {}
    n_prev = 0
    if kv_prev is not None:
        in_specs += [pl.BlockSpec(memory_space=pl.ANY)] * 2
        args += list(kv_prev)
        aliases = {4: 1, 5: 2}
        n_prev = 2
    cast_in, cast_out, cast_shapes = _row_chunk_specs(cast_src, layer, n_tiles)
    args += list(cast_src)
    table_in, table_out, table_shape = [], [], []
    if rpb_flat is not None:
        assert n_tiles == N_HEADS
        table_in = [pl.BlockSpec(memory_space=pltpu.SMEM)]
        table_out = [pl.BlockSpec(TABLE_BLOCK, lambda i: (i, 0, 0, 0))]
        table_shape = [jax.ShapeDtypeStruct((N_HEADS,) + TABLE_BLOCK[1:], F32)]
        args.append(rpb_flat)
    return pl.pallas_call(
        functools.partial(_inproj_kernel, mod_row0=mod_row0, tiles_per_seq=tps, seqs_per_tile=spt,
                          n_prev=n_prev, n_cast=len(cast_src), fuse_attn=fuse_attn,
                          table_layer=None if rpb_flat is None else layer),
        grid=(n_tiles,),
        in_specs=in_specs + cast_in + table_in,
        out_specs=[
            tok(ATTN_WIDTH) if fuse_attn else pl.BlockSpec(head_blk, head_map),
            pl.BlockSpec(kv_blk, kv_map),
            pl.BlockSpec(kv_blk, kv_map),
            tok(CONV_WIDTH), tok(CONV_WIDTH), tok(D_MODEL), tok(D_MODEL),
        ] + cast_out + table_out,
        out_shape=[
            (tok_shape(ATTN_WIDTH) if fuse_attn
             else jax.ShapeDtypeStruct((n_seq, N_HEADS, seq_len, HEAD_DIM), BF16)),
            kv_shape, kv_shape,
            tok_shape(CONV_WIDTH), tok_shape(CONV_WIDTH), tok_shape(D_MODEL), tok_shape(D_MODEL),
        ] + cast_shapes + table_shape,
        input_output_aliases=aliases,
        compiler_params=_params(1),
        name="inproj_ctx" if is_ctx else "inproj_lat",
    )(*args)


def _software_pipeline(n_items, score, normalise, finish):
    scores = {0: score(0)}
    probs = {}
    for i in range(n_items + 1):
        if i + 1 < n_items:
            scores[i + 1] = score(i + 1)
        if i < n_items:
            probs[i] = normalise(i, scores.pop(i))
        if i >= 1:
            finish(i - 1, probs.pop(i - 1))


HEADS_PER_STEP = 2


def _lat_attn_kernel(q_ref, kt_ref, vt_ref, kct_ref, vct_ref, tab_ref, o_ref, *, n_rows):
    n_chunks = n_rows // ROWS_PER_CHUNK
    chunk_q = ROWS_PER_CHUNK * GRID_W
    union = UNION_ROWS * GRID_W
    items = [(c, hh) for c in range(n_chunks) for hh in range(HEADS_PER_STEP)]
    done = {}

    def key_cols(c):
        first_key_row = min(max(c * ROWS_PER_CHUNK - WIN_R // 2, 0), n_rows - UNION_ROWS)
        return slice(first_key_row * GRID_W, first_key_row * GRID_W + union)

    def score(i):
        c, hh = items[i]
        kind = 0 if c == 0 else (2 if c == n_chunks - 1 else 1)
        qc = q_ref[0, hh, c * chunk_q:(c + 1) * chunk_q, :]
        s_band = _dot(qc, kt_ref[0, 0, hh, :, key_cols(c)]) + tab_ref[hh, kind]
        s_ctx = _dot(qc, kct_ref[0, 0, hh].astype(BF16))
        return s_band, s_ctx

    def normalise(i, scores):
        s_band, s_ctx = scores
        m = jnp.maximum(jnp.max(s_band, axis=-1, keepdims=True),
                        jnp.max(s_ctx, axis=-1, keepdims=True))
        p_band = jnp.exp(s_band - m)
        p_ctx = jnp.exp(s_ctx - m)
        denom = jnp.sum(p_band, axis=-1, keepdims=True) + jnp.sum(p_ctx, axis=-1, keepdims=True)
        return p_band.astype(BF16), p_ctx.astype(BF16), denom

    def finish(i, prob):
        c, hh = items[i]
        p_band, p_ctx, denom = prob
        o = (_dot_nt(p_band, vt_ref[0, 0, hh, :, key_cols(c)])
             + _dot_nt(p_ctx, vct_ref[0, 0, hh].astype(BF16)))
        done[hh] = o / denom
        if hh == HEADS_PER_STEP - 1:
            o_ref[0, c * chunk_q:(c + 1) * chunk_q, :] = jnp.concatenate(
                [done.pop(h) for h in range(HEADS_PER_STEP)], axis=-1).astype(BF16)

    _software_pipeline(len(items), score, normalise, finish)


def _lat_attention(q, k_t, v_t, cache_k_t, cache_v_t, tables, layer):
    n_seq, _, seq_len, _ = q.shape
    past_len = cache_k_t.shape[4]
    n_rows = seq_len // GRID_W
    assert n_rows % ROWS_PER_CHUNK == 0 and n_rows >= 2 * UNION_ROWS
    hps = HEADS_PER_STEP
    return pl.pallas_call(
        functools.partial(_lat_attn_kernel, n_rows=n_rows),
        grid=(N_HEADS // hps, n_seq),
        in_specs=[
            pl.BlockSpec((1, hps, seq_len, HEAD_DIM), lambda p, b: (b, p, 0, 0)),
            pl.BlockSpec((1, 1, hps, HEAD_DIM, seq_len), lambda p, b: (b, 0, p, 0, 0)),
            pl.BlockSpec((1, 1, hps, HEAD_DIM, seq_len), lambda p, b: (b, 0, p, 0, 0)),
            pl.BlockSpec((1, 1, hps, HEAD_DIM, past_len), lambda p, b: (b, layer, p, 0, 0)),
            pl.BlockSpec((1, 1, hps, HEAD_DIM, past_len), lambda p, b: (b, layer, p, 0, 0)),
            pl.BlockSpec((hps,) + tables.shape[1:], lambda p, b: (p, 0, 0, 0)),
        ],
        out_specs=pl.BlockSpec((1, seq_len, hps * HEAD_DIM), lambda p, b: (b, 0, p)),
        out_shape=jax.ShapeDtypeStruct((n_seq, seq_len, ATTN_WIDTH), BF16),
        compiler_params=_params(2),
        name="attn_lat",
    )(q, k_t, v_t, cache_k_t, cache_v_t, tables)


def _layer_norm(y, g, b):
    mu = jnp.mean(y, axis=-1, keepdims=True)
    yc = y - mu
    var = jnp.mean(yc * yc, axis=-1, keepdims=True)
    return yc * lax.rsqrt(var + LN_EPS) * g + b


def _post_kernel(x_ref, attn_ref, z_ref, zprev_ref, znext_ref, bg_ref, ga_ref, gc_ref, mod_ref,
                 wap_ref, wcp_ref, wo_ref, w1_ref, w2_ref, cw_ref, cb_ref,
                 ln1g_ref, ln1b_ref, b1_ref, b2_ref, ln2g_ref, ln2b_ref, *rest,
                 mod_row0, tiles_per_seq, seq_len, n_cast):
    o_ref = rest[n_cast]
    _cast_rows(rest[:n_cast], rest[n_cast + 1:])
    i = pl.program_id(0)
    row = _cond_row(i, mod_row0, tiles_per_seq)
    d = D_MODEL
    g1 = mod_ref[0, pl.ds(row, 1), 2 * d:3 * d]
    sh2 = mod_ref[0, pl.ds(row, 1), 3 * d:4 * d]
    sc2 = mod_ref[0, pl.ds(row, 1), 4 * d:5 * d]
    g2 = mod_ref[0, pl.ds(row, 1), 5 * d:6 * d]

    z = z_ref[...].astype(F32)
    prev_row = zprev_ref[...].astype(F32)[HALO_ROWS - 1:HALO_ROWS, :]
    next_row = znext_ref[...].astype(F32)[0:1, :]
    pos = lax.broadcasted_iota(jnp.int32, z.shape, 0)
    seq_pos = (i * TOKEN_TILE + pos) & (seq_len - 1)
    z_before = jnp.where(pos == 0, prev_row, pltpu.roll(z, 1, 0))
    z_before = jnp.where(seq_pos == 0, 0.0, z_before)
    z_after = jnp.where(pos == TOKEN_TILE - 1, next_row, pltpu.roll(z, TOKEN_TILE - 1, 0))
    z_after = jnp.where(seq_pos == seq_len - 1, 0.0, z_after)
    conv = z_before * cw_ref[0:1, :] + z * cw_ref[1:2, :] + z_after * cw_ref[2:3, :] + cb_ref[...]
    conv_in = (bg_ref[...].astype(F32) * conv).astype(BF16)

    def merge(rs):
        attn_p = _dot(attn_ref[rs, :], wap_ref[...])
        conv_p = _dot(conv_in[rs, :], wcp_ref[...])
        return (jax.nn.sigmoid(ga_ref[rs, :].astype(F32)) * attn_p
                + jax.nn.sigmoid(gc_ref[rs, :].astype(F32)) * conv_p).astype(BF16)

    def mix_norm(merged, rs):
        mix = _dot(merged, wo_ref[...])
        x1 = _layer_norm(ALPHA * x_ref[rs, :] + g1 * mix, ln1g_ref[...], ln1b_ref[...])
        return x1, (x1 * (1.0 + sc2) + sh2).astype(BF16)

    def mlp_chunk(h2, c):
        cols = slice(c * FF_CHUNK, (c + 1) * FF_CHUNK)
        hidden = jnp.maximum(_dot(h2, w1_ref[:, cols]) + b1_ref[:, cols], 0.0)
        return _dot((hidden * hidden).astype(BF16), w2_ref[cols, :])

    def finish(x1, f, rs):
        o_ref[rs, :] = _layer_norm(ALPHA * x1 + g2 * f, ln2g_ref[...], ln2b_ref[...])

    groups = [slice(r0, r0 + POST_ROW_GROUP) for r0 in range(0, TOKEN_TILE, POST_ROW_GROUP)]
    n_chunks = D_FF // FF_CHUNK
    x1, h2 = mix_norm(merge(groups[0]), groups[0])
    f = b2_ref[...] + mlp_chunk(h2, 0)
    for n, rs in enumerate(groups):
        nxt = groups[n + 1] if n + 1 < len(groups) else None
        if nxt is not None:
            merged_next = merge(nxt)
        f = f + mlp_chunk(h2, 1)
        if nxt is not None:
            x1_next, h2_next = mix_norm(merged_next, nxt)
        for c in range(2, n_chunks):
            f = f + mlp_chunk(h2, c)
        if nxt is not None:
            f_next = b2_ref[...] + mlp_chunk(h2_next, 0)
        finish(x1, f, rs)
        if nxt is not None:
            x1, h2, f = x1_next, h2_next, f_next


def _post(x, attn, z, bg, ga, gc, mod, mats, weights, layer, *, seq_len, mod_row0, cast_src=(),
          cast_layer=None):
    n_tok = x.shape[0]
    assert seq_len & (seq_len - 1) == 0
    tps = max(seq_len // TOKEN_TILE, 1)
    n_tiles = n_tok // TOKEN_TILE
    halo_per_tile = TOKEN_TILE // HALO_ROWS
    n_halo_blocks = n_tok // HALO_ROWS
    tok = lambda width: pl.BlockSpec((TOKEN_TILE, width), lambda i: (i, 0))
    in_specs = [
        tok(D_MODEL), tok(ATTN_WIDTH), tok(CONV_WIDTH),
        pl.BlockSpec((HALO_ROWS, CONV_WIDTH), lambda i: (jnp.maximum(i * halo_per_tile - 1, 0), 0)),
        pl.BlockSpec((HALO_ROWS, CONV_WIDTH),
                     lambda i: (jnp.minimum((i + 1) * halo_per_tile, n_halo_blocks - 1), 0)),
        tok(CONV_WIDTH), tok(D_MODEL), tok(D_MODEL),
        pl.BlockSpec((1, COND_ROWS, 6 * D_MODEL), lambda i: (layer, 0, 0)),
    ]
    per_layer = [weights["conv_w"]] + weights["vectors"]
    in_specs += [_resident(m.shape) for m in mats]
    in_specs += [_layer_resident(w.shape[1:], layer) for w in per_layer]
    cast_in, cast_out, cast_shapes = _row_chunk_specs(cast_src, cast_layer, n_tiles)
    return pl.pallas_call(
        functools.partial(_post_kernel, mod_row0=mod_row0, tiles_per_seq=tps, seq_len=seq_len,
                          n_cast=len(cast_src)),
        grid=(n_tiles,),
        in_specs=in_specs + cast_in,
        out_specs=[tok(D_MODEL)] + cast_out,
        out_shape=[jax.ShapeDtypeStruct((n_tok, D_MODEL), F32)] + cast_shapes,
        compiler_params=_params(1),
        name="post_ctx" if mod_row0 == 0 else "post_lat",
    )(x, attn, z, z, z, bg, ga, gc, mod, *mats, *per_layer, *cast_src)


def kernel(x_prompt, x_sample, cache_k, cache_v, c, c_ctx, w_mod, b_mod, w_in, rpb, conv_w, conv_b,
           w_attn_proj, w_conv_proj, w_o, ln1_g, ln1_b, w1, b1, w2, b2, ln2_g, ln2_b):
    batch, seq, _ = x_prompt.shape
    dec_batch, dec_seq, _ = x_sample.shape

    cond = jnp.concatenate(
        [c_ctx[None, :], c, jnp.zeros((COND_ROWS - 1 - dec_batch, D_MODEL), F32)], axis=0)
    mod = _modulation(cond, w_mod, b_mod)
    rpb_flat = rpb.reshape(-1)

    w_in_l = w_in[0].astype(BF16)
    post_mats_f32 = [w_attn_proj, w_conv_proj, w_o, w1, w2]
    w_kv_t = _transposed_kv_weights(w_in)
    cache_k_t = jnp.swapaxes(cache_k, -1, -2)
    cache_v_t = jnp.swapaxes(cache_v, -1, -2)
    weights = {
        "conv_w": conv_w,
        "vectors": [v.reshape(DEPTH, 1, -1) for v in (conv_b, ln1_g, ln1_b, b1, b2, ln2_g, ln2_b)],
    }

    xp = x_prompt.reshape(batch * seq, D_MODEL)
    xs = x_sample.reshape(dec_batch * dec_seq, D_MODEL)
    kv = None
    for layer in range(DEPTH):
        attn, new_k_t, new_v_t, z, bg, ga, gc, *mats = _inproj(
            xp, mod, w_in_l, w_kv_t, layer, n_seq=batch, seq_len=seq, mod_row0=0, kv_prev=kv,
            cast_src=post_mats_f32)
        kv = (new_k_t, new_v_t)
        last = layer == DEPTH - 1
        xp, *w_in_next = _post(xp, attn, z, bg, ga, gc, mod, mats, weights, layer, seq_len=seq,
                               mod_row0=0, cast_src=() if last else (w_in,), cast_layer=layer + 1)

        q, k_t, v_t, z, bg, ga, gc, tables = _inproj(
            xs, mod, w_in_l, w_kv_t, layer, n_seq=dec_batch, seq_len=dec_seq, mod_row0=1,
            rpb_flat=rpb_flat)
        attn = _lat_attention(q, k_t, v_t, cache_k_t, cache_v_t, tables, layer)
        (xs,) = _post(xs, attn.reshape(dec_batch * dec_seq, ATTN_WIDTH), z, bg, ga, gc, mod, mats,
                      weights, layer, seq_len=dec_seq, mod_row0=1)
        if not last:
            (w_in_l,) = w_in_next

    new_k = jnp.swapaxes(kv[0], -1, -2)
    new_v = jnp.swapaxes(kv[1], -1, -2)
    return (xp.reshape(batch, seq, D_MODEL), xs.reshape(dec_batch, dec_seq, D_MODEL), new_k, new_v)
```

```python
import functools

import jax
import jax.numpy as jnp
from jax import lax
from jax.experimental import pallas as pl
from jax.experimental.pallas import tpu as pltpu

D_MODEL = 1024
DEPTH = 4
N_HEADS = 8
HEAD_DIM = 64
ATTN_WIDTH = N_HEADS * HEAD_DIM
CONV_WIDTH = D_MODEL // 2
D_FF = 4 * D_MODEL
GRID_W = 64
WIN_R = 8
WIN_C = 16
RPB_ROWS = 2 * WIN_R - 1
RPB_COLS = 2 * WIN_C - 1
IN_COLS = 3 * ATTN_WIDTH + 3 * CONV_WIDTH + 2 * D_MODEL
SCALE = HEAD_DIM ** -0.5
ALPHA = (2.0 * DEPTH) ** 0.25
LN_EPS = 1e-5

COND_ROWS = 8
INPROJ_TILE = 512
TOKEN_TILE = 512
FF_CHUNK = 1024
POST_ROW_GROUP = 256
HALO_ROWS = 16
ROWS_PER_CHUNK = 4
VMEM_LIMIT_BYTES = 56 * 1024 * 1024

BF16 = jnp.bfloat16
F32 = jnp.float32
NT_DIMS = (((1,), (1,)), ((), ()))


def _dot(a, b):
    return jnp.dot(a, b, preferred_element_type=F32)


def _dot_nt(a, b):
    return lax.dot_general(a, b, NT_DIMS, preferred_element_type=F32)


def _resident(shape):
    zeros = (0,) * len(shape)
    return pl.BlockSpec(tuple(shape), lambda *_: zeros, pipeline_mode=pl.Buffered(1))


def _layer_resident(shape, layer):
    zeros = (0,) * len(shape)
    return pl.BlockSpec((None,) + tuple(shape), lambda *_: (layer,) + zeros,
                        pipeline_mode=pl.Buffered(1))


def _cond_row(tile, mod_row0, tiles_per_seq):
    if mod_row0 == 0:
        return 0
    return mod_row0 + tile // tiles_per_seq


def _params(n_grid_dims):
    return pltpu.CompilerParams(
        dimension_semantics=("arbitrary",) * n_grid_dims,
        vmem_limit_bytes=VMEM_LIMIT_BYTES,
    )


def _cast_rows(src_refs, dst_refs):
    for src, dst in zip(src_refs, dst_refs, strict=True):
        dst[...] = src[...].astype(BF16)


def _row_chunk_specs(stacked, layer, n_steps):
    in_specs, out_specs, out_shapes = [], [], []
    for w in stacked:
        _, n_rows, n_cols = w.shape
        chunk = n_rows // n_steps
        assert chunk * n_steps == n_rows and chunk % 16 == 0
        in_specs.append(pl.BlockSpec((None, chunk, n_cols), lambda i: (layer, i, 0)))
        out_specs.append(pl.BlockSpec((chunk, n_cols), lambda i: (i, 0)))
        out_shapes.append(jax.ShapeDtypeStruct((n_rows, n_cols), BF16))
    return in_specs, out_specs, out_shapes


MOD_COL_TILE = 1536


def _mod_kernel(cond_ref, w_ref, b_ref, o_ref):
    c = cond_ref[...]
    s = (c * jax.nn.sigmoid(c)).astype(BF16)
    o_ref[0] = _dot(s, w_ref[0].astype(BF16)) + b_ref[pl.ds(pl.program_id(0), 1), :]


def _modulation(cond, w_mod, b_mod):
    n_cols = 6 * D_MODEL
    return pl.pallas_call(
        _mod_kernel,
        grid=(DEPTH, n_cols // MOD_COL_TILE),
        in_specs=[
            pl.BlockSpec((COND_ROWS, D_MODEL), lambda l, j: (0, 0)),
            pl.BlockSpec((1, D_MODEL, MOD_COL_TILE), lambda l, j: (l, 0, j)),
            pl.BlockSpec((DEPTH, MOD_COL_TILE), lambda l, j: (0, j)),
        ],
        out_specs=pl.BlockSpec((1, COND_ROWS, MOD_COL_TILE), lambda l, j: (l, 0, j)),
        out_shape=jax.ShapeDtypeStruct((DEPTH, COND_ROWS, n_cols), F32),
        compiler_params=_params(2),
        name="modulation",
    )(cond, w_mod, b_mod)


UNION_ROWS = WIN_R + ROWS_PER_CHUNK
N_CHUNK_KINDS = 3
assert ROWS_PER_CHUNK == WIN_R // 2


def _write_score_tables(rpb_ref, base, o_ref):
    shape = (GRID_W, 2 * GRID_W)
    q = lax.broadcasted_iota(jnp.int32, shape, 0)
    lane = lax.broadcasted_iota(jnp.int32, shape, 1)
    kc = lane & (GRID_W - 1)
    col_start = jnp.clip(q - WIN_C // 2, 0, GRID_W - WIN_C)
    valid = (kc >= col_start) & (kc < col_start + WIN_C)
    neg_inf = jnp.full(shape, -jnp.inf, F32)
    sub = lax.broadcasted_iota(jnp.int32, (8, 2 * GRID_W), 0)
    lane8 = lax.broadcasted_iota(jnp.int32, (8, 2 * GRID_W), 1)
    pad = GRID_W // 2
    t = lane8 - pad - sub + (WIN_C - 1)
    per_row = []
    for j in range(RPB_ROWS):
        tile = jnp.full((8, 2 * GRID_W), -jnp.inf, F32)
        for m in range(RPB_COLS):
            tile = jnp.where(t == m, rpb_ref[base + j * RPB_COLS + m], tile)
        groups = []
        for a in range(GRID_W // 8):
            low = pltpu.roll(tile, (8 * a - pad) % (2 * GRID_W), 1)
            high = pltpu.roll(tile, 8 * a - pad + GRID_W, 1)
            groups.append(jnp.where(lane8 < GRID_W, low, high))
        per_row.append(jnp.where(valid, jnp.concatenate(groups, axis=0), neg_inf))
    chunk_kinds = (
        [(0, g) for g in range(ROWS_PER_CHUNK)],
        [(g, WIN_R // 2) for g in range(ROWS_PER_CHUNK)],
        [(ROWS_PER_CHUNK, WIN_R // 2 + g) for g in range(ROWS_PER_CHUNK)],
    )
    for kind, rows in enumerate(chunk_kinds):
        for g, (offset, d) in enumerate(rows):
            for pair in range(UNION_ROWS // 2):
                halves = []
                for j in (2 * pair, 2 * pair + 1):
                    i = j - offset
                    halves.append(per_row[i + WIN_R - 1 - d] if 0 <= i < WIN_R else neg_inf)
                o_ref[0, kind, g * GRID_W:(g + 1) * GRID_W, pair * 128:(pair + 1) * 128] = (
                    jnp.where(lane < GRID_W, halves[0], halves[1]))


TABLE_BLOCK = (1, N_CHUNK_KINDS, ROWS_PER_CHUNK * GRID_W, UNION_ROWS * GRID_W)


def _transpose_cast_kernel(w_ref, o_ref):
    o_ref[...] = w_ref[...].T.astype(BF16)


def _transposed_kv_weights(w_in):
    return pl.pallas_call(
        _transpose_cast_kernel,
        grid=(DEPTH, 2),
        in_specs=[pl.BlockSpec((None, D_MODEL, ATTN_WIDTH), lambda l, j: (l, 0, 1 + j))],
        out_specs=pl.BlockSpec((None, ATTN_WIDTH, D_MODEL), lambda l, j: (l, j, 0)),
        out_shape=jax.ShapeDtypeStruct((DEPTH, 2 * ATTN_WIDTH, D_MODEL), BF16),
        compiler_params=_params(2),
        name="kv_weight_transpose",
    )(w_in)


def _inproj_kernel(x_ref, mod_ref, w_ref, wkv_t_ref, *rest, mod_row0, tiles_per_seq, seqs_per_tile,
                   n_prev, n_cast, fuse_attn, table_layer):
    cast_src = rest[n_prev:n_prev + n_cast]
    n_in = n_prev + n_cast + (table_layer is not None)
    outs = rest[n_in:]
    first_ref, kt_ref, vt_ref, z_ref, bg_ref, ga_ref, gc_ref = outs[:7]
    _cast_rows(cast_src, outs[7:7 + n_cast])
    i = pl.program_id(0)
    row = _cond_row(i, mod_row0, tiles_per_seq)
    sh1 = mod_ref[0, pl.ds(row, 1), 0:D_MODEL]
    sc1 = mod_ref[0, pl.ds(row, 1), D_MODEL:2 * D_MODEL]
    h = (x_ref[...] * (1.0 + sc1) + sh1).astype(BF16)

    def proj(lo, hi):
        return _dot(h, w_ref[:, lo:hi])

    a = ATTN_WIDTH
    c0 = 3 * ATTN_WIDTH
    g0 = c0 + 3 * CONV_WIDTH
    rows = INPROJ_TILE // seqs_per_tile
    head = lambda hd: slice(hd * HEAD_DIM, (hd + 1) * HEAD_DIM)
    seq = lambda s: slice(s * rows, (s + 1) * rows)

    q = proj(0, a) * SCALE
    kv_t = []
    for lo, ref in ((0, kt_ref), (a, vt_ref)):
        y_t = _dot_nt(wkv_t_ref[lo:lo + a, :], h)
        kv_t.append(y_t)
        for s in range(seqs_per_tile):
            for hd in range(N_HEADS):
                ref[s, 0, hd] = y_t[head(hd), seq(s)].astype(ref.dtype)

    def gate_a():
        ga_ref[...] = proj(g0, g0 + D_MODEL).astype(BF16)

    def gate_c():
        gc_ref[...] = proj(g0 + D_MODEL, g0 + 2 * D_MODEL).astype(BF16)

    def conv_input():
        u = proj(c0, c0 + CONV_WIDTH)
        cg = proj(c0 + 2 * CONV_WIDTH, c0 + 3 * CONV_WIDTH)
        z_ref[...] = (cg * u).astype(BF16)

    def conv_gate():
        bg_ref[...] = proj(c0 + CONV_WIDTH, c0 + 2 * CONV_WIDTH).astype(BF16)

    def score_tables():
        base = (table_layer * N_HEADS + i) * (RPB_ROWS * RPB_COLS)
        _write_score_tables(rest[n_in - 1], base, outs[-1])

    remaining = [gate_a, gate_c, conv_input, conv_gate]
    if table_layer is not None:
        remaining.insert(1, score_tables)
    if not fuse_attn:
        for s in range(seqs_per_tile):
            for hd in range(N_HEADS):
                first_ref[s, hd] = q[seq(s), head(hd)].astype(BF16)
        for emit in remaining:
            emit()
        return

    items = [(s, hd) for s in range(seqs_per_tile) for hd in range(N_HEADS)]
    every = len(items) // len(remaining)
    done = {}

    def score(n):
        s, hd = items[n]
        sc = _dot(q[seq(s), head(hd)].astype(BF16), kv_t[0][head(hd), seq(s)].astype(BF16))
        if n % every == 1:
            remaining.pop(0)()
        return sc

    def normalise(n, sc):
        p = jnp.exp(sc - jnp.max(sc, axis=-1, keepdims=True))
        return p.astype(BF16), jnp.sum(p, axis=-1, keepdims=True)

    def finish(n, prob):
        s, hd = items[n]
        p, denom = prob
        done[hd] = _dot_nt(p, kv_t[1][head(hd), seq(s)].astype(BF16)) / denom
        if hd % 2 == 1:
            first_ref[seq(s), (hd - 1) * HEAD_DIM:(hd + 1) * HEAD_DIM] = jnp.concatenate(
                [done.pop(hd - 1), done.pop(hd)], axis=-1).astype(BF16)

    _software_pipeline(len(items), score, normalise, finish)
    assert not remaining


def _inproj(x, mod, w_in_l, w_kv_t, layer, *, n_seq, seq_len, mod_row0, kv_prev=None, cast_src=(),
            rpb_flat=None):
    n_tok = n_seq * seq_len
    n_tiles = n_tok // INPROJ_TILE
    is_ctx = mod_row0 == 0
    if seq_len >= INPROJ_TILE:
        tps, spt, rows = seq_len // INPROJ_TILE, 1, INPROJ_TILE
        head_map = lambda i: (i // tps, 0, i % tps, 0)
        kv_map = lambda i: (i // tps, layer if is_ctx else 0, 0, 0, i % tps)
    else:
        tps, spt, rows = 1, INPROJ_TILE // seq_len, seq_len
        head_map = lambda i: (i, 0, 0, 0)
        kv_map = lambda i: (i, layer if is_ctx else 0, 0, 0, 0)
    head_blk = (spt, N_HEADS, rows, HEAD_DIM)
    fuse_attn = is_ctx and seq_len <= INPROJ_TILE
    kv_blk = (spt, 1, N_HEADS, HEAD_DIM, rows)
    if is_ctx:
        kv_shape = jax.ShapeDtypeStruct((n_seq, DEPTH, N_HEADS, HEAD_DIM, seq_len), F32)
    else:
        kv_shape = jax.ShapeDtypeStruct((n_seq, 1, N_HEADS, HEAD_DIM, seq_len), BF16)
    tok = lambda width: pl.BlockSpec((INPROJ_TILE, width), lambda i: (i, 0))
    tok_shape = lambda width: jax.ShapeDtypeStruct((n_tok, width), BF16)
    in_specs = [
        tok(D_MODEL),
        pl.BlockSpec((1, COND_ROWS, 6 * D_MODEL), lambda i: (layer, 0, 0)),
        _resident((D_MODEL, IN_COLS)),
        _layer_resident((2 * ATTN_WIDTH, D_MODEL), layer),
    ]
    args = [x, mod, w_in_l, w_kv_t]
    aliases = {}
    n_prev = 0
    if kv_prev is not None:
        in_specs += [pl.BlockSpec(memory_space=pl.ANY)] * 2
        args += list(kv_prev)
        aliases = {4: 1, 5: 2}
        n_prev = 2
    cast_in, cast_out, cast_shapes = _row_chunk_specs(cast_src, layer, n_tiles)
    args += list(cast_src)
    table_in, table_out, table_shape = [], [], []
    if rpb_flat is not None:
        assert n_tiles == N_HEADS
        table_in = [pl.BlockSpec(memory_space=pltpu.SMEM)]
        table_out = [pl.BlockSpec(TABLE_BLOCK, lambda i: (i, 0, 0, 0))]
        table_shape = [jax.ShapeDtypeStruct((N_HEADS,) + TABLE_BLOCK[1:], F32)]
        args.append(rpb_flat)
    return pl.pallas_call(
        functools.partial(_inproj_kernel, mod_row0=mod_row0, tiles_per_seq=tps, seqs_per_tile=spt,
                          n_prev=n_prev, n_cast=len(cast_src), fuse_attn=fuse_attn,
                          table_layer=None if rpb_flat is None else layer),
        grid=(n_tiles,),
        in_specs=in_specs + cast_in + table_in,
        out_specs=[
            tok(ATTN_WIDTH) if fuse_attn else pl.BlockSpec(head_blk, head_map),
            pl.BlockSpec(kv_blk, kv_map),
            pl.BlockSpec(kv_blk, kv_map),
            tok(CONV_WIDTH), tok(CONV_WIDTH), tok(D_MODEL), tok(D_MODEL),
        ] + cast_out + table_out,
        out_shape=[
            (tok_shape(ATTN_WIDTH) if fuse_attn
             else jax.ShapeDtypeStruct((n_seq, N_HEADS, seq_len, HEAD_DIM), BF16)),
            kv_shape, kv_shape,
            tok_shape(CONV_WIDTH), tok_shape(CONV_WIDTH), tok_shape(D_MODEL), tok_shape(D_MODEL),
        ] + cast_shapes + table_shape,
        input_output_aliases=aliases,
        compiler_params=_params(1),
        name="inproj_ctx" if is_ctx else "inproj_lat",
    )(*args)


def _software_pipeline(n_items, score, normalise, finish):
    scores = {0: score(0)}
    probs = {}
    for i in range(n_items + 1):
        if i + 1 < n_items:
            scores[i + 1] = score(i + 1)
        if i < n_items:
            probs[i] = normalise(i, scores.pop(i))
        if i >= 1:
            finish(i - 1, probs.pop(i - 1))


HEADS_PER_STEP = 2


def _lat_attn_kernel(q_ref, kt_ref, vt_ref, kct_ref, vct_ref, tab_ref, o_ref, *, n_rows):
    n_chunks = n_rows // ROWS_PER_CHUNK
    chunk_q = ROWS_PER_CHUNK * GRID_W
    union = UNION_ROWS * GRID_W
    items = [(c, hh) for c in range(n_chunks) for hh in range(HEADS_PER_STEP)]
    done = {}

    def key_cols(c):
        first_key_row = min(max(c * ROWS_PER_CHUNK - WIN_R // 2, 0), n_rows - UNION_ROWS)
        return slice(first_key_row * GRID_W, first_key_row * GRID_W + union)

    def score(i):
        c, hh = items[i]
        kind = 0 if c == 0 else (2 if c == n_chunks - 1 else 1)
        qc = q_ref[0, hh, c * chunk_q:(c + 1) * chunk_q, :]
        s_band = _dot(qc, kt_ref[0, 0, hh, :, key_cols(c)]) + tab_ref[hh, kind]
        s_ctx = _dot(qc, kct_ref[0, 0, hh].astype(BF16))
        return s_band, s_ctx

    def normalise(i, scores):
        s_band, s_ctx = scores
        m = jnp.maximum(jnp.max(s_band, axis=-1, keepdims=True),
                        jnp.max(s_ctx, axis=-1, keepdims=True))
        p_band = jnp.exp(s_band - m)
        p_ctx = jnp.exp(s_ctx - m)
        denom = jnp.sum(p_band, axis=-1, keepdims=True) + jnp.sum(p_ctx, axis=-1, keepdims=True)
        return p_band.astype(BF16), p_ctx.astype(BF16), denom

    def finish(i, prob):
        c, hh = items[i]
        p_band, p_ctx, denom = prob
        o = (_dot_nt(p_band, vt_ref[0, 0, hh, :, key_cols(c)])
             + _dot_nt(p_ctx, vct_ref[0, 0, hh].astype(BF16)))
        done[hh] = o / denom
        if hh == HEADS_PER_STEP - 1:
            o_ref[0, c * chunk_q:(c + 1) * chunk_q, :] = jnp.concatenate(
                [done.pop(h) for h in range(HEADS_PER_STEP)], axis=-1).astype(BF16)

    _software_pipeline(len(items), score, normalise, finish)


def _lat_attention(q, k_t, v_t, cache_k_t, cache_v_t, tables, layer):
    n_seq, _, seq_len, _ = q.shape
    past_len = cache_k_t.shape[4]
    n_rows = seq_len // GRID_W
    assert n_rows % ROWS_PER_CHUNK == 0 and n_rows >= 2 * UNION_ROWS
    hps = HEADS_PER_STEP
    return pl.pallas_call(
        functools.partial(_lat_attn_kernel, n_rows=n_rows),
        grid=(N_HEADS // hps, n_seq),
        in_specs=[
            pl.BlockSpec((1, hps, seq_len, HEAD_DIM), lambda p, b: (b, p, 0, 0)),
            pl.BlockSpec((1, 1, hps, HEAD_DIM, seq_len), lambda p, b: (b, 0, p, 0, 0)),
            pl.BlockSpec((1, 1, hps, HEAD_DIM, seq_len), lambda p, b: (b, 0, p, 0, 0)),
            pl.BlockSpec((1, 1, hps, HEAD_DIM, past_len), lambda p, b: (b, layer, p, 0, 0)),
            pl.BlockSpec((1, 1, hps, HEAD_DIM, past_len), lambda p, b: (b, layer, p, 0, 0)),
            pl.BlockSpec((hps,) + tables.shape[1:], lambda p, b: (p, 0, 0, 0)),
        ],
        out_specs=pl.BlockSpec((1, seq_len, hps * HEAD_DIM), lambda p, b: (b, 0, p)),
        out_shape=jax.ShapeDtypeStruct((n_seq, seq_len, ATTN_WIDTH), BF16),
        compiler_params=_params(2),
        name="attn_lat",
    )(q, k_t, v_t, cache_k_t, cache_v_t, tables)


def _layer_norm(y, g, b):
    mu = jnp.mean(y, axis=-1, keepdims=True)
    yc = y - mu
    var = jnp.mean(yc * yc, axis=-1, keepdims=True)
    return yc * lax.rsqrt(var + LN_EPS) * g + b


def _post_kernel(x_ref, attn_ref, z_ref, zprev_ref, znext_ref, bg_ref, ga_ref, gc_ref, mod_ref,
                 wap_ref, wcp_ref, wo_ref, w1_ref, w2_ref, cw_ref, cb_ref,
                 ln1g_ref, ln1b_ref, b1_ref, b2_ref, ln2g_ref, ln2b_ref, *rest,
                 mod_row0, tiles_per_seq, seq_len, n_cast, layer):
    o_ref = rest[n_cast]
    _cast_rows(rest[:n_cast], rest[n_cast + 1:])
    cb_ref, ln1g_ref, ln1b_ref, b1_ref, b2_ref, ln2g_ref, ln2b_ref = (
        r.at[layer:layer + 1] for r in (cb_ref, ln1g_ref, ln1b_ref, b1_ref, b2_ref, ln2g_ref,
                                        ln2b_ref))
    i = pl.program_id(0)
    row = _cond_row(i, mod_row0, tiles_per_seq)
    d = D_MODEL
    g1 = mod_ref[0, pl.ds(row, 1), 2 * d:3 * d]
    sh2 = mod_ref[0, pl.ds(row, 1), 3 * d:4 * d]
    sc2 = mod_ref[0, pl.ds(row, 1), 4 * d:5 * d]
    g2 = mod_ref[0, pl.ds(row, 1), 5 * d:6 * d]

    z = z_ref[...].astype(F32)
    prev_row = zprev_ref[...].astype(F32)[HALO_ROWS - 1:HALO_ROWS, :]
    next_row = znext_ref[...].astype(F32)[0:1, :]
    pos = lax.broadcasted_iota(jnp.int32, z.shape, 0)
    seq_pos = (i * TOKEN_TILE + pos) & (seq_len - 1)
    z_before = jnp.where(pos == 0, prev_row, pltpu.roll(z, 1, 0))
    z_before = jnp.where(seq_pos == 0, 0.0, z_before)
    z_after = jnp.where(pos == TOKEN_TILE - 1, next_row, pltpu.roll(z, TOKEN_TILE - 1, 0))
    z_after = jnp.where(seq_pos == seq_len - 1, 0.0, z_after)
    conv = z_before * cw_ref[0:1, :] + z * cw_ref[1:2, :] + z_after * cw_ref[2:3, :] + cb_ref[...]
    conv_in = (bg_ref[...].astype(F32) * conv).astype(BF16)

    def merge(rs):
        attn_p = _dot(attn_ref[rs, :], wap_ref[...])
        conv_p = _dot(conv_in[rs, :], wcp_ref[...])
        return (jax.nn.sigmoid(ga_ref[rs, :].astype(F32)) * attn_p
                + jax.nn.sigmoid(gc_ref[rs, :].astype(F32)) * conv_p).astype(BF16)

    def mix_norm(merged, rs):
        mix = _dot(merged, wo_ref[...])
        x1 = _layer_norm(ALPHA * x_ref[rs, :] + g1 * mix, ln1g_ref[...], ln1b_ref[...])
        return x1, (x1 * (1.0 + sc2) + sh2).astype(BF16)

    def mlp_chunk(h2, c):
        cols = slice(c * FF_CHUNK, (c + 1) * FF_CHUNK)
        hidden = jnp.maximum(_dot(h2, w1_ref[:, cols]) + b1_ref[:, cols], 0.0)
        return _dot((hidden * hidden).astype(BF16), w2_ref[cols, :])

    def finish(x1, f, rs):
        o_ref[rs, :] = _layer_norm(ALPHA * x1 + g2 * f, ln2g_ref[...], ln2b_ref[...])

    groups = [slice(r0, r0 + POST_ROW_GROUP) for r0 in range(0, TOKEN_TILE, POST_ROW_GROUP)]
    n_chunks = D_FF // FF_CHUNK
    x1, h2 = mix_norm(merge(groups[0]), groups[0])
    f = b2_ref[...] + mlp_chunk(h2, 0)
    for n, rs in enumerate(groups):
        nxt = groups[n + 1] if n + 1 < len(groups) else None
        if nxt is not None:
            merged_next = merge(nxt)
        f = f + mlp_chunk(h2, 1)
        if nxt is not None:
            x1_next, h2_next = mix_norm(merged_next, nxt)
        for c in range(2, n_chunks):
            f = f + mlp_chunk(h2, c)
        if nxt is not None:
            f_next = b2_ref[...] + mlp_chunk(h2_next, 0)
        finish(x1, f, rs)
        if nxt is not None:
            x1, h2, f = x1_next, h2_next, f_next


def _post(x, attn, z, bg, ga, gc, mod, mats, weights, layer, *, seq_len, mod_row0, cast_src=(),
          cast_layer=None):
    n_tok = x.shape[0]
    assert seq_len & (seq_len - 1) == 0
    tps = max(seq_len // TOKEN_TILE, 1)
    n_tiles = n_tok // TOKEN_TILE
    halo_per_tile = TOKEN_TILE // HALO_ROWS
    n_halo_blocks = n_tok // HALO_ROWS
    tok = lambda width: pl.BlockSpec((TOKEN_TILE, width), lambda i: (i, 0))
    in_specs = [
        tok(D_MODEL), tok(ATTN_WIDTH), tok(CONV_WIDTH),
        pl.BlockSpec((HALO_ROWS, CONV_WIDTH), lambda i: (jnp.maximum(i * halo_per_tile - 1, 0), 0)),
        pl.BlockSpec((HALO_ROWS, CONV_WIDTH),
                     lambda i: (jnp.minimum((i + 1) * halo_per_tile, n_halo_blocks - 1), 0)),
        tok(CONV_WIDTH), tok(D_MODEL), tok(D_MODEL),
        pl.BlockSpec((1, COND_ROWS, 6 * D_MODEL), lambda i: (layer, 0, 0)),
    ]
    per_layer = [weights["conv_w"]] + weights["vectors"]
    in_specs += [_resident(m.shape) for m in mats]
    in_specs += [_layer_resident(weights["conv_w"].shape[1:], layer)]
    in_specs += [_resident(v.shape) for v in weights["vectors"]]
    cast_in, cast_out, cast_shapes = _row_chunk_specs(cast_src, cast_layer, n_tiles)
    return pl.pallas_call(
        functools.partial(_post_kernel, mod_row0=mod_row0, tiles_per_seq=tps, seq_len=seq_len,
                          n_cast=len(cast_src), layer=layer),
        grid=(n_tiles,),
        in_specs=in_specs + cast_in,
        out_specs=[tok(D_MODEL)] + cast_out,
        out_shape=[jax.ShapeDtypeStruct((n_tok, D_MODEL), F32)] + cast_shapes,
        compiler_params=_params(1),
        name="post_ctx" if mod_row0 == 0 else "post_lat",
    )(x, attn, z, z, z, bg, ga, gc, mod, *mats, *per_layer, *cast_src)


def kernel(x_prompt, x_sample, cache_k, cache_v, c, c_ctx, w_mod, b_mod, w_in, rpb, conv_w, conv_b,
           w_attn_proj, w_conv_proj, w_o, ln1_g, ln1_b, w1, b1, w2, b2, ln2_g, ln2_b):
    batch, seq, _ = x_prompt.shape
    dec_batch, dec_seq, _ = x_sample.shape

    cond = jnp.concatenate(
        [c_ctx[None, :], c, jnp.zeros((COND_ROWS - 1 - dec_batch, D_MODEL), F32)], axis=0)
    mod = _modulation(cond, w_mod, b_mod)
    rpb_flat = rpb.reshape(-1)

    w_in_l = w_in[0].astype(BF16)
    post_mats_f32 = [w_attn_proj, w_conv_proj, w_o, w1, w2]
    w_kv_t = _transposed_kv_weights(w_in)
    cache_k_t = jnp.swapaxes(cache_k, -1, -2)
    cache_v_t = jnp.swapaxes(cache_v, -1, -2)
    weights = {
        "conv_w": conv_w,
        "vectors": [conv_b, ln1_g, ln1_b, b1, b2, ln2_g, ln2_b],
    }

    xp = x_prompt.reshape(batch * seq, D_MODEL)
    xs = x_sample.reshape(dec_batch * dec_seq, D_MODEL)
    kv = None
    for layer in range(DEPTH):
        attn, new_k_t, new_v_t, z, bg, ga, gc, *mats = _inproj(
            xp, mod, w_in_l, w_kv_t, layer, n_seq=batch, seq_len=seq, mod_row0=0, kv_prev=kv,
            cast_src=post_mats_f32)
        kv = (new_k_t, new_v_t)
        last = layer == DEPTH - 1
        xp, *w_in_next = _post(xp, attn, z, bg, ga, gc, mod, mats, weights, layer, seq_len=seq,
                               mod_row0=0, cast_src=() if last else (w_in,), cast_layer=layer + 1)

        q, k_t, v_t, z, bg, ga, gc, tables = _inproj(
            xs, mod, w_in_l, w_kv_t, layer, n_seq=dec_batch, seq_len=dec_seq, mod_row0=1,
            rpb_flat=rpb_flat)
        attn = _lat_attention(q, k_t, v_t, cache_k_t, cache_v_t, tables, layer)
        (xs,) = _post(xs, attn.reshape(dec_batch * dec_seq, ATTN_WIDTH), z, bg, ga, gc, mod, mats,
                      weights, layer, seq_len=dec_seq, mod_row0=1)
        if not last:
            (w_in_l,) = w_in_next

    new_k = jnp.swapaxes(kv[0], -1, -2)
    new_v = jnp.swapaxes(kv[1], -1, -2)
    return (xp.reshape(batch, seq, D_MODEL), xs.reshape(dec_batch, dec_seq, D_MODEL), new_k, new_v)
```

```python
import functools

import jax
import jax.numpy as jnp
from jax import lax
from jax.experimental import pallas as pl
from jax.experimental.pallas import tpu as pltpu

D_MODEL = 1024
DEPTH = 4
N_HEADS = 8
HEAD_DIM = 64
ATTN_WIDTH = N_HEADS * HEAD_DIM
CONV_WIDTH = D_MODEL // 2
D_FF = 4 * D_MODEL
GRID_W = 64
WIN_R = 8
WIN_C = 16
RPB_ROWS = 2 * WIN_R - 1
RPB_COLS = 2 * WIN_C - 1
IN_COLS = 3 * ATTN_WIDTH + 3 * CONV_WIDTH + 2 * D_MODEL
SCALE = HEAD_DIM ** -0.5
ALPHA = (2.0 * DEPTH) ** 0.25
LN_EPS = 1e-5

COND_ROWS = 8
INPROJ_TILE = 512
TOKEN_TILE = 512
FF_CHUNK = 1024
POST_ROW_GROUP = 256
HALO_ROWS = 16
ROWS_PER_CHUNK = 4
VMEM_LIMIT_BYTES = 56 * 1024 * 1024

BF16 = jnp.bfloat16
F32 = jnp.float32
NT_DIMS = (((1,), (1,)), ((), ()))


def _dot(a, b):
    return jnp.dot(a, b, preferred_element_type=F32)


def _dot_nt(a, b):
    return lax.dot_general(a, b, NT_DIMS, preferred_element_type=F32)


def _resident(shape):
    zeros = (0,) * len(shape)
    return pl.BlockSpec(tuple(shape), lambda *_: zeros, pipeline_mode=pl.Buffered(1))


def _layer_resident(shape, layer):
    zeros = (0,) * len(shape)
    return pl.BlockSpec((None,) + tuple(shape), lambda *_: (layer,) + zeros,
                        pipeline_mode=pl.Buffered(1))


def _cond_row(tile, mod_row0, tiles_per_seq):
    if mod_row0 == 0:
        return 0
    return mod_row0 + tile // tiles_per_seq


def _params(n_grid_dims):
    return pltpu.CompilerParams(
        dimension_semantics=("arbitrary",) * n_grid_dims,
        vmem_limit_bytes=VMEM_LIMIT_BYTES,
    )


def _cast_rows(src_refs, dst_refs):
    for src, dst in zip(src_refs, dst_refs, strict=True):
        dst[...] = src[...].astype(BF16)


def _row_chunk_specs(stacked, layer, n_steps):
    in_specs, out_specs, out_shapes = [], [], []
    for w in stacked:
        _, n_rows, n_cols = w.shape
        chunk = n_rows // n_steps
        assert chunk * n_steps == n_rows and chunk % 16 == 0
        in_specs.append(pl.BlockSpec((None, chunk, n_cols), lambda i: (layer, i, 0)))
        out_specs.append(pl.BlockSpec((chunk, n_cols), lambda i: (i, 0)))
        out_shapes.append(jax.ShapeDtypeStruct((n_rows, n_cols), BF16))
    return in_specs, out_specs, out_shapes


MOD_COL_TILE = 1536


def _mod_kernel(cond_ref, w_ref, b_ref, o_ref):
    c = cond_ref[...]
    s = (c * jax.nn.sigmoid(c)).astype(BF16)
    o_ref[0] = _dot(s, w_ref[0].astype(BF16)) + b_ref[pl.ds(pl.program_id(0), 1), :]


def _modulation(cond, w_mod, b_mod):
    n_cols = 6 * D_MODEL
    return pl.pallas_call(
        _mod_kernel,
        grid=(DEPTH, n_cols // MOD_COL_TILE),
        in_specs=[
            pl.BlockSpec((COND_ROWS, D_MODEL), lambda l, j: (0, 0)),
            pl.BlockSpec((1, D_MODEL, MOD_COL_TILE), lambda l, j: (l, 0, j)),
            pl.BlockSpec((DEPTH, MOD_COL_TILE), lambda l, j: (0, j)),
        ],
        out_specs=pl.BlockSpec((1, COND_ROWS, MOD_COL_TILE), lambda l, j: (l, 0, j)),
        out_shape=jax.ShapeDtypeStruct((DEPTH, COND_ROWS, n_cols), F32),
        compiler_params=_params(2),
        name="modulation",
    )(cond, w_mod, b_mod)


UNION_ROWS = WIN_R + ROWS_PER_CHUNK
N_CHUNK_KINDS = 3
assert ROWS_PER_CHUNK == WIN_R // 2


def _write_score_tables(rpb_ref, base, o_ref):
    shape = (GRID_W, 2 * GRID_W)
    q = lax.broadcasted_iota(jnp.int32, shape, 0)
    lane = lax.broadcasted_iota(jnp.int32, shape, 1)
    kc = lane & (GRID_W - 1)
    col_start = jnp.clip(q - WIN_C // 2, 0, GRID_W - WIN_C)
    valid = (kc >= col_start) & (kc < col_start + WIN_C)
    neg_inf = jnp.full(shape, -jnp.inf, F32)
    sub = lax.broadcasted_iota(jnp.int32, (8, 2 * GRID_W), 0)
    lane8 = lax.broadcasted_iota(jnp.int32, (8, 2 * GRID_W), 1)
    pad = GRID_W // 2
    t = lane8 - pad - sub + (WIN_C - 1)
    per_row = []
    for j in range(RPB_ROWS):
        tile = jnp.full((8, 2 * GRID_W), -jnp.inf, F32)
        for m in range(RPB_COLS):
            tile = jnp.where(t == m, rpb_ref[base + j * RPB_COLS + m], tile)
        groups = []
        for a in range(GRID_W // 8):
            low = pltpu.roll(tile, (8 * a - pad) % (2 * GRID_W), 1)
            high = pltpu.roll(tile, 8 * a - pad + GRID_W, 1)
            groups.append(jnp.where(lane8 < GRID_W, low, high))
        per_row.append(jnp.where(valid, jnp.concatenate(groups, axis=0), neg_inf))
    chunk_kinds = (
        [(0, g) for g in range(ROWS_PER_CHUNK)],
        [(g, WIN_R // 2) for g in range(ROWS_PER_CHUNK)],
        [(ROWS_PER_CHUNK, WIN_R // 2 + g) for g in range(ROWS_PER_CHUNK)],
    )
    for kind, rows in enumerate(chunk_kinds):
        for g, (offset, d) in enumerate(rows):
            for pair in range(UNION_ROWS // 2):
                halves = []
                for j in (2 * pair, 2 * pair + 1):
                    i = j - offset
                    halves.append(per_row[i + WIN_R - 1 - d] if 0 <= i < WIN_R else neg_inf)
                o_ref[0, kind, g * GRID_W:(g + 1) * GRID_W, pair * 128:(pair + 1) * 128] = (
                    jnp.where(lane < GRID_W, halves[0], halves[1]))


TABLE_BLOCK = (1, N_CHUNK_KINDS, ROWS_PER_CHUNK * GRID_W, UNION_ROWS * GRID_W)


def _transpose_cast_kernel(w_ref, o_ref):
    o_ref[...] = w_ref[...].T.astype(BF16)


def _transposed_kv_weights(w_in):
    return pl.pallas_call(
        _transpose_cast_kernel,
        grid=(DEPTH, 2),
        in_specs=[pl.BlockSpec((None, D_MODEL, ATTN_WIDTH), lambda l, j: (l, 0, 1 + j))],
        out_specs=pl.BlockSpec((None, ATTN_WIDTH, D_MODEL), lambda l, j: (l, j, 0)),
        out_shape=jax.ShapeDtypeStruct((DEPTH, 2 * ATTN_WIDTH, D_MODEL), BF16),
        compiler_params=_params(2),
        name="kv_weight_transpose",
    )(w_in)


def _inproj_kernel(x_ref, mod_ref, w_ref, wkv_t_ref, *rest, mod_row0, tiles_per_seq, seqs_per_tile,
                   n_prev, n_cast, fuse_attn, table_layer, layer):
    cast_src = rest[n_prev:n_prev + n_cast]
    n_in = n_prev + n_cast + (table_layer is not None) + 2 * fuse_attn
    outs = rest[n_in:]
    if fuse_attn:
        first_ref, kt_ref, vt_ref, z_ref, ga_ref, gc_ref = outs[:6]
        n_main = 6
    else:
        first_ref, kt_ref, vt_ref, z_ref, bg_ref, ga_ref, gc_ref = outs[:7]
        n_main = 7
    _cast_rows(cast_src, outs[n_main:n_main + n_cast])
    i = pl.program_id(0)
    row = _cond_row(i, mod_row0, tiles_per_seq)
    sh1 = mod_ref[0, pl.ds(row, 1), 0:D_MODEL]
    sc1 = mod_ref[0, pl.ds(row, 1), D_MODEL:2 * D_MODEL]
    h = (x_ref[...] * (1.0 + sc1) + sh1).astype(BF16)

    def proj(lo, hi):
        return _dot(h, w_ref[:, lo:hi])

    a = ATTN_WIDTH
    c0 = 3 * ATTN_WIDTH
    g0 = c0 + 3 * CONV_WIDTH
    rows = INPROJ_TILE // seqs_per_tile
    head = lambda hd: slice(hd * HEAD_DIM, (hd + 1) * HEAD_DIM)
    seq = lambda s: slice(s * rows, (s + 1) * rows)

    q = proj(0, a) * SCALE
    kv_t = []
    for lo, ref in ((0, kt_ref), (a, vt_ref)):
        y_t = _dot_nt(wkv_t_ref[lo:lo + a, :], h)
        kv_t.append(y_t)
        for s in range(seqs_per_tile):
            for hd in range(N_HEADS):
                ref[s, 0, hd] = y_t[head(hd), seq(s)].astype(ref.dtype)

    def gate_a():
        ga_ref[...] = proj(g0, g0 + D_MODEL).astype(BF16)

    def gate_c():
        gc_ref[...] = proj(g0 + D_MODEL, g0 + 2 * D_MODEL).astype(BF16)

    def conv_input():
        u = proj(c0, c0 + CONV_WIDTH)
        cg = proj(c0 + 2 * CONV_WIDTH, c0 + 3 * CONV_WIDTH)
        z_ref[...] = (cg * u).astype(BF16)

    def conv_gate():
        bg_ref[...] = proj(c0 + CONV_WIDTH, c0 + 2 * CONV_WIDTH).astype(BF16)

    def score_tables():
        base = (table_layer * N_HEADS + i) * (RPB_ROWS * RPB_COLS)
        _write_score_tables(rest[n_in - 1], base, outs[-1])

    def gated_conv():
        cw_ref, cb_ref = rest[n_in - 2], rest[n_in - 1].at[layer:layer + 1]
        z = proj(c0 + 2 * CONV_WIDTH, c0 + 3 * CONV_WIDTH) * proj(c0, c0 + CONV_WIDTH)
        seq_pos = lax.broadcasted_iota(jnp.int32, z.shape, 0) & (rows - 1)
        z_before = jnp.where(seq_pos == 0, 0.0, pltpu.roll(z, 1, 0))
        z_after = jnp.where(seq_pos == rows - 1, 0.0, pltpu.roll(z, INPROJ_TILE - 1, 0))
        conv = (z_before * cw_ref[0:1, :] + z * cw_ref[1:2, :] + z_after * cw_ref[2:3, :]
                + cb_ref[...])
        z_ref[...] = (proj(c0 + CONV_WIDTH, c0 + 2 * CONV_WIDTH) * conv).astype(BF16)

    if fuse_attn:
        remaining = [gate_a, gate_c, gated_conv]
    else:
        remaining = [gate_a, gate_c, conv_input, conv_gate]
    if table_layer is not None:
        remaining.insert(1, score_tables)
    if not fuse_attn:
        for s in range(seqs_per_tile):
            for hd in range(N_HEADS):
                first_ref[s, hd] = q[seq(s), head(hd)].astype(BF16)
        for emit in remaining:
            emit()
        return

    items = [(s, hd) for s in range(seqs_per_tile) for hd in range(N_HEADS)]
    every = len(items) // len(remaining)
    done = {}

    def score(n):
        s, hd = items[n]
        sc = _dot(q[seq(s), head(hd)].astype(BF16), kv_t[0][head(hd), seq(s)].astype(BF16))
        if n % every == 1:
            remaining.pop(0)()
        return sc

    def normalise(n, sc):
        p = jnp.exp(sc - jnp.max(sc, axis=-1, keepdims=True))
        return p.astype(BF16), jnp.sum(p, axis=-1, keepdims=True)

    def finish(n, prob):
        s, hd = items[n]
        p, denom = prob
        done[hd] = _dot_nt(p, kv_t[1][head(hd), seq(s)].astype(BF16)) / denom
        if hd % 2 == 1:
            first_ref[seq(s), (hd - 1) * HEAD_DIM:(hd + 1) * HEAD_DIM] = jnp.concatenate(
                [done.pop(hd - 1), done.pop(hd)], axis=-1).astype(BF16)

    _software_pipeline(len(items), score, normalise, finish)
    assert not remaining


def _inproj(x, mod, w_in_l, w_kv_t, layer, *, n_seq, seq_len, mod_row0, kv_prev=None, cast_src=(),
            rpb_flat=None, conv=None):
    n_tok = n_seq * seq_len
    n_tiles = n_tok // INPROJ_TILE
    is_ctx = mod_row0 == 0
    if seq_len >= INPROJ_TILE:
        tps, spt, rows = seq_len // INPROJ_TILE, 1, INPROJ_TILE
        head_map = lambda i: (i // tps, 0, i % tps, 0)
        kv_map = lambda i: (i // tps, layer if is_ctx else 0, 0, 0, i % tps)
    else:
        tps, spt, rows = 1, INPROJ_TILE // seq_len, seq_len
        head_map = lambda i: (i, 0, 0, 0)
        kv_map = lambda i: (i, layer if is_ctx else 0, 0, 0, 0)
    head_blk = (spt, N_HEADS, rows, HEAD_DIM)
    fuse_attn = is_ctx and seq_len <= INPROJ_TILE
    kv_blk = (spt, 1, N_HEADS, HEAD_DIM, rows)
    if is_ctx:
        kv_shape = jax.ShapeDtypeStruct((n_seq, DEPTH, N_HEADS, HEAD_DIM, seq_len), F32)
    else:
        kv_shape = jax.ShapeDtypeStruct((n_seq, 1, N_HEADS, HEAD_DIM, seq_len), BF16)
    tok = lambda width: pl.BlockSpec((INPROJ_TILE, width), lambda i: (i, 0))
    tok_shape = lambda width: jax.ShapeDtypeStruct((n_tok, width), BF16)
    in_specs = [
        tok(D_MODEL),
        pl.BlockSpec((1, COND_ROWS, 6 * D_MODEL), lambda i: (layer, 0, 0)),
        _resident((D_MODEL, IN_COLS)),
        _layer_resident((2 * ATTN_WIDTH, D_MODEL), layer),
    ]
    args = [x, mod, w_in_l, w_kv_t]
    aliases = {}
    n_prev = 0
    if kv_prev is not None:
        in_specs += [pl.BlockSpec(memory_space=pl.ANY)] * 2
        args += list(kv_prev)
        aliases = {4: 1, 5: 2}
        n_prev = 2
    cast_in, cast_out, cast_shapes = _row_chunk_specs(cast_src, layer, n_tiles)
    args += list(cast_src)
    table_in, table_out, table_shape = [], [], []
    if rpb_flat is not None:
        assert n_tiles == N_HEADS
        table_in = [pl.BlockSpec(memory_space=pltpu.SMEM)]
        table_out = [pl.BlockSpec(TABLE_BLOCK, lambda i: (i, 0, 0, 0))]
        table_shape = [jax.ShapeDtypeStruct((N_HEADS,) + TABLE_BLOCK[1:], F32)]
        args.append(rpb_flat)
    conv_in = []
    if fuse_attn:
        assert rows & (rows - 1) == 0 and conv is not None
        conv_w, conv_b = conv
        conv_in = [_layer_resident(conv_w.shape[1:], layer), _resident(conv_b.shape)]
        args += [conv_w, conv_b]
        main_specs = [tok(ATTN_WIDTH), pl.BlockSpec(kv_blk, kv_map), pl.BlockSpec(kv_blk, kv_map),
                      tok(CONV_WIDTH), tok(D_MODEL), tok(D_MODEL)]
        main_shapes = [tok_shape(ATTN_WIDTH), kv_shape, kv_shape,
                       tok_shape(CONV_WIDTH), tok_shape(D_MODEL), tok_shape(D_MODEL)]
    else:
        main_specs = [pl.BlockSpec(head_blk, head_map), pl.BlockSpec(kv_blk, kv_map),
                      pl.BlockSpec(kv_blk, kv_map),
                      tok(CONV_WIDTH), tok(CONV_WIDTH), tok(D_MODEL), tok(D_MODEL)]
        main_shapes = [jax.ShapeDtypeStruct((n_seq, N_HEADS, seq_len, HEAD_DIM), BF16),
                       kv_shape, kv_shape, tok_shape(CONV_WIDTH), tok_shape(CONV_WIDTH),
                       tok_shape(D_MODEL), tok_shape(D_MODEL)]
    return pl.pallas_call(
        functools.partial(_inproj_kernel, mod_row0=mod_row0, tiles_per_seq=tps, seqs_per_tile=spt,
                          n_prev=n_prev, n_cast=len(cast_src), fuse_attn=fuse_attn,
                          table_layer=None if rpb_flat is None else layer, layer=layer),
        grid=(n_tiles,),
        in_specs=in_specs + cast_in + table_in + conv_in,
        out_specs=main_specs + cast_out + table_out,
        out_shape=main_shapes + cast_shapes + table_shape,
        input_output_aliases=aliases,
        compiler_params=_params(1),
        name="inproj_ctx" if is_ctx else "inproj_lat",
    )(*args)


def _software_pipeline(n_items, score, normalise, finish):
    scores = {0: score(0)}
    probs = {}
    for i in range(n_items + 1):
        if i + 1 < n_items:
            scores[i + 1] = score(i + 1)
        if i < n_items:
            probs[i] = normalise(i, scores.pop(i))
        if i >= 1:
            finish(i - 1, probs.pop(i - 1))


HEADS_PER_STEP = 2


def _lat_attn_kernel(q_ref, kt_ref, vt_ref, kct_ref, vct_ref, tab_ref, o_ref, *, n_rows):
    n_chunks = n_rows // ROWS_PER_CHUNK
    chunk_q = ROWS_PER_CHUNK * GRID_W
    union = UNION_ROWS * GRID_W
    items = [(c, hh) for c in range(n_chunks) for hh in range(HEADS_PER_STEP)]
    done = {}

    def key_cols(c):
        first_key_row = min(max(c * ROWS_PER_CHUNK - WIN_R // 2, 0), n_rows - UNION_ROWS)
        return slice(first_key_row * GRID_W, first_key_row * GRID_W + union)

    def score(i):
        c, hh = items[i]
        kind = 0 if c == 0 else (2 if c == n_chunks - 1 else 1)
        qc = q_ref[0, hh, c * chunk_q:(c + 1) * chunk_q, :]
        s_band = _dot(qc, kt_ref[0, 0, hh, :, key_cols(c)]) + tab_ref[hh, kind]
        s_ctx = _dot(qc, kct_ref[0, 0, hh].astype(BF16))
        return s_band, s_ctx

    def normalise(i, scores):
        s_band, s_ctx = scores
        m = jnp.maximum(jnp.max(s_band, axis=-1, keepdims=True),
                        jnp.max(s_ctx, axis=-1, keepdims=True))
        p_band = jnp.exp(s_band - m)
        p_ctx = jnp.exp(s_ctx - m)
        denom = jnp.sum(p_band, axis=-1, keepdims=True) + jnp.sum(p_ctx, axis=-1, keepdims=True)
        return p_band.astype(BF16), p_ctx.astype(BF16), denom

    def finish(i, prob):
        c, hh = items[i]
        p_band, p_ctx, denom = prob
        o = (_dot_nt(p_band, vt_ref[0, 0, hh, :, key_cols(c)])
             + _dot_nt(p_ctx, vct_ref[0, 0, hh].astype(BF16)))
        done[hh] = o / denom
        if hh == HEADS_PER_STEP - 1:
            o_ref[0, c * chunk_q:(c + 1) * chunk_q, :] = jnp.concatenate(
                [done.pop(h) for h in range(HEADS_PER_STEP)], axis=-1).astype(BF16)

    _software_pipeline(len(items), score, normalise, finish)


def _lat_attention(q, k_t, v_t, cache_k_t, cache_v_t, tables, layer):
    n_seq, _, seq_len, _ = q.shape
    past_len = cache_k_t.shape[4]
    n_rows = seq_len // GRID_W
    assert n_rows % ROWS_PER_CHUNK == 0 and n_rows >= 2 * UNION_ROWS
    hps = HEADS_PER_STEP
    return pl.pallas_call(
        functools.partial(_lat_attn_kernel, n_rows=n_rows),
        grid=(N_HEADS // hps, n_seq),
        in_specs=[
            pl.BlockSpec((1, hps, seq_len, HEAD_DIM), lambda p, b: (b, p, 0, 0)),
            pl.BlockSpec((1, 1, hps, HEAD_DIM, seq_len), lambda p, b: (b, 0, p, 0, 0)),
            pl.BlockSpec((1, 1, hps, HEAD_DIM, seq_len), lambda p, b: (b, 0, p, 0, 0)),
            pl.BlockSpec((1, 1, hps, HEAD_DIM, past_len), lambda p, b: (b, layer, p, 0, 0)),
            pl.BlockSpec((1, 1, hps, HEAD_DIM, past_len), lambda p, b: (b, layer, p, 0, 0)),
            pl.BlockSpec((hps,) + tables.shape[1:], lambda p, b: (p, 0, 0, 0)),
        ],
        out_specs=pl.BlockSpec((1, seq_len, hps * HEAD_DIM), lambda p, b: (b, 0, p)),
        out_shape=jax.ShapeDtypeStruct((n_seq, seq_len, ATTN_WIDTH), BF16),
        compiler_params=_params(2),
        name="attn_lat",
    )(q, k_t, v_t, cache_k_t, cache_v_t, tables)


def _layer_norm(y, g, b):
    mu = jnp.mean(y, axis=-1, keepdims=True)
    yc = y - mu
    var = jnp.mean(yc * yc, axis=-1, keepdims=True)
    return yc * lax.rsqrt(var + LN_EPS) * g + b


def _post_kernel(x_ref, attn_ref, z_ref, zprev_ref, znext_ref, bg_ref, ga_ref, gc_ref, mod_ref,
                 wap_ref, wcp_ref, wo_ref, w1_ref, w2_ref, cw_ref, cb_ref,
                 ln1g_ref, ln1b_ref, b1_ref, b2_ref, ln2g_ref, ln2b_ref, *rest,
                 mod_row0, tiles_per_seq, seq_len, n_cast, layer, conv_done):
    o_ref = rest[n_cast]
    _cast_rows(rest[:n_cast], rest[n_cast + 1:])
    cb_ref, ln1g_ref, ln1b_ref, b1_ref, b2_ref, ln2g_ref, ln2b_ref = (
        r.at[layer:layer + 1] for r in (cb_ref, ln1g_ref, ln1b_ref, b1_ref, b2_ref, ln2g_ref,
                                        ln2b_ref))
    i = pl.program_id(0)
    row = _cond_row(i, mod_row0, tiles_per_seq)
    d = D_MODEL
    g1 = mod_ref[0, pl.ds(row, 1), 2 * d:3 * d]
    sh2 = mod_ref[0, pl.ds(row, 1), 3 * d:4 * d]
    sc2 = mod_ref[0, pl.ds(row, 1), 4 * d:5 * d]
    g2 = mod_ref[0, pl.ds(row, 1), 5 * d:6 * d]

    if conv_done:
        conv_in = z_ref[...]
    else:
        z = z_ref[...].astype(F32)
        prev_row = zprev_ref[...].astype(F32)[HALO_ROWS - 1:HALO_ROWS, :]
        next_row = znext_ref[...].astype(F32)[0:1, :]
        pos = lax.broadcasted_iota(jnp.int32, z.shape, 0)
        seq_pos = (i * TOKEN_TILE + pos) & (seq_len - 1)
        z_before = jnp.where(pos == 0, prev_row, pltpu.roll(z, 1, 0))
        z_before = jnp.where(seq_pos == 0, 0.0, z_before)
        z_after = jnp.where(pos == TOKEN_TILE - 1, next_row, pltpu.roll(z, TOKEN_TILE - 1, 0))
        z_after = jnp.where(seq_pos == seq_len - 1, 0.0, z_after)
        conv = (z_before * cw_ref[0:1, :] + z * cw_ref[1:2, :] + z_after * cw_ref[2:3, :]
                + cb_ref[...])
        conv_in = (bg_ref[...].astype(F32) * conv).astype(BF16)

    def merge(rs):
        attn_p = _dot(attn_ref[rs, :], wap_ref[...])
        conv_p = _dot(conv_in[rs, :], wcp_ref[...])
        return (jax.nn.sigmoid(ga_ref[rs, :].astype(F32)) * attn_p
                + jax.nn.sigmoid(gc_ref[rs, :].astype(F32)) * conv_p).astype(BF16)

    def mix_norm(merged, rs):
        mix = _dot(merged, wo_ref[...])
        x1 = _layer_norm(ALPHA * x_ref[rs, :] + g1 * mix, ln1g_ref[...], ln1b_ref[...])
        return x1, (x1 * (1.0 + sc2) + sh2).astype(BF16)

    def mlp_chunk(h2, c):
        cols = slice(c * FF_CHUNK, (c + 1) * FF_CHUNK)
        hidden = jnp.maximum(_dot(h2, w1_ref[:, cols]) + b1_ref[:, cols], 0.0)
        return _dot((hidden * hidden).astype(BF16), w2_ref[cols, :])

    def finish(x1, f, rs):
        o_ref[rs, :] = _layer_norm(ALPHA * x1 + g2 * f, ln2g_ref[...], ln2b_ref[...])

    groups = [slice(r0, r0 + POST_ROW_GROUP) for r0 in range(0, TOKEN_TILE, POST_ROW_GROUP)]
    n_chunks = D_FF // FF_CHUNK
    x1, h2 = mix_norm(merge(groups[0]), groups[0])
    f = b2_ref[...] + mlp_chunk(h2, 0)
    for n, rs in enumerate(groups):
        nxt = groups[n + 1] if n + 1 < len(groups) else None
        if nxt is not None:
            merged_next = merge(nxt)
        f = f + mlp_chunk(h2, 1)
        if nxt is not None:
            x1_next, h2_next = mix_norm(merged_next, nxt)
        for c in range(2, n_chunks):
            f = f + mlp_chunk(h2, c)
        if nxt is not None:
            f_next = b2_ref[...] + mlp_chunk(h2_next, 0)
        finish(x1, f, rs)
        if nxt is not None:
            x1, h2, f = x1_next, h2_next, f_next


def _post(x, attn, z, bg, ga, gc, mod, mats, weights, layer, *, seq_len, mod_row0, cast_src=(),
          cast_layer=None):
    n_tok = x.shape[0]
    conv_done = bg is None
    if conv_done:
        bg = z
    assert seq_len & (seq_len - 1) == 0
    tps = max(seq_len // TOKEN_TILE, 1)
    n_tiles = n_tok // TOKEN_TILE
    halo_per_tile = TOKEN_TILE // HALO_ROWS
    n_halo_blocks = n_tok // HALO_ROWS
    tok = lambda width: pl.BlockSpec((TOKEN_TILE, width), lambda i: (i, 0))
    in_specs = [
        tok(D_MODEL), tok(ATTN_WIDTH), tok(CONV_WIDTH),
        pl.BlockSpec((HALO_ROWS, CONV_WIDTH), lambda i: (jnp.maximum(i * halo_per_tile - 1, 0), 0)),
        pl.BlockSpec((HALO_ROWS, CONV_WIDTH),
                     lambda i: (jnp.minimum((i + 1) * halo_per_tile, n_halo_blocks - 1), 0)),
        tok(CONV_WIDTH), tok(D_MODEL), tok(D_MODEL),
        pl.BlockSpec((1, COND_ROWS, 6 * D_MODEL), lambda i: (layer, 0, 0)),
    ]
    per_layer = [weights["conv_w"]] + weights["vectors"]
    in_specs += [_resident(m.shape) for m in mats]
    in_specs += [_layer_resident(weights["conv_w"].shape[1:], layer)]
    in_specs += [_resident(v.shape) for v in weights["vectors"]]
    cast_in, cast_out, cast_shapes = _row_chunk_specs(cast_src, cast_layer, n_tiles)
    return pl.pallas_call(
        functools.partial(_post_kernel, mod_row0=mod_row0, tiles_per_seq=tps, seq_len=seq_len,
                          n_cast=len(cast_src), layer=layer, conv_done=conv_done),
        grid=(n_tiles,),
        in_specs=in_specs + cast_in,
        out_specs=[tok(D_MODEL)] + cast_out,
        out_shape=[jax.ShapeDtypeStruct((n_tok, D_MODEL), F32)] + cast_shapes,
        compiler_params=_params(1),
        name="post_ctx" if mod_row0 == 0 else "post_lat",
    )(x, attn, z, z, z, bg, ga, gc, mod, *mats, *per_layer, *cast_src)


def kernel(x_prompt, x_sample, cache_k, cache_v, c, c_ctx, w_mod, b_mod, w_in, rpb, conv_w, conv_b,
           w_attn_proj, w_conv_proj, w_o, ln1_g, ln1_b, w1, b1, w2, b2, ln2_g, ln2_b):
    batch, seq, _ = x_prompt.shape
    dec_batch, dec_seq, _ = x_sample.shape

    cond = jnp.concatenate(
        [c_ctx[None, :], c, jnp.zeros((COND_ROWS - 1 - dec_batch, D_MODEL), F32)], axis=0)
    mod = _modulation(cond, w_mod, b_mod)
    rpb_flat = rpb.reshape(-1)

    w_in_l = w_in[0].astype(BF16)
    post_mats_f32 = [w_attn_proj, w_conv_proj, w_o, w1, w2]
    w_kv_t = _transposed_kv_weights(w_in)
    cache_k_t = jnp.swapaxes(cache_k, -1, -2)
    cache_v_t = jnp.swapaxes(cache_v, -1, -2)
    weights = {
        "conv_w": conv_w,
        "vectors": [conv_b, ln1_g, ln1_b, b1, b2, ln2_g, ln2_b],
    }

    xp = x_prompt.reshape(batch * seq, D_MODEL)
    xs = x_sample.reshape(dec_batch * dec_seq, D_MODEL)
    kv = None
    for layer in range(DEPTH):
        attn, new_k_t, new_v_t, gated_conv, ga, gc, *mats = _inproj(
            xp, mod, w_in_l, w_kv_t, layer, n_seq=batch, seq_len=seq, mod_row0=0, kv_prev=kv,
            cast_src=post_mats_f32, conv=(conv_w, conv_b))
        kv = (new_k_t, new_v_t)
        last = layer == DEPTH - 1
        xp, *w_in_next = _post(xp, attn, gated_conv, None, ga, gc, mod, mats, weights, layer,
                               seq_len=seq, mod_row0=0, cast_src=() if last else (w_in,),
                               cast_layer=layer + 1)

        q, k_t, v_t, z, bg, ga, gc, tables = _inproj(
            xs, mod, w_in_l, w_kv_t, layer, n_seq=dec_batch, seq_len=dec_seq, mod_row0=1,
            rpb_flat=rpb_flat)
        attn = _lat_attention(q, k_t, v_t, cache_k_t, cache_v_t, tables, layer)
        (xs,) = _post(xs, attn.reshape(dec_batch * dec_seq, ATTN_WIDTH), z, bg, ga, gc, mod, mats,
                      weights, layer, seq_len=dec_seq, mod_row0=1)
        if not last:
            (w_in_l,) = w_in_next

    new_k = jnp.swapaxes(kv[0], -1, -2)
    new_v = jnp.swapaxes(kv[1], -1, -2)
    return (xp.reshape(batch, seq, D_MODEL), xs.reshape(dec_batch, dec_seq, D_MODEL), new_k, new_v)
```

```python
import functools

import jax
import jax.numpy as jnp
from jax import lax
from jax.experimental import pallas as pl
from jax.experimental.pallas import tpu as pltpu

D_MODEL = 1024
DEPTH = 4
N_HEADS = 8
HEAD_DIM = 64
ATTN_WIDTH = N_HEADS * HEAD_DIM
CONV_WIDTH = D_MODEL // 2
D_FF = 4 * D_MODEL
GRID_W = 64
WIN_R = 8
WIN_C = 16
RPB_ROWS = 2 * WIN_R - 1
RPB_COLS = 2 * WIN_C - 1
IN_COLS = 3 * ATTN_WIDTH + 3 * CONV_WIDTH + 2 * D_MODEL
SCALE = HEAD_DIM ** -0.5
ALPHA = (2.0 * DEPTH) ** 0.25
LN_EPS = 1e-5

COND_ROWS = 8
INPROJ_TILE = 512
TOKEN_TILE = 512
FF_CHUNK = 1024
POST_ROW_GROUP = 256
GATE_PIECE = 256
HALO_ROWS = 16
ROWS_PER_CHUNK = 4
VMEM_LIMIT_BYTES = 56 * 1024 * 1024

BF16 = jnp.bfloat16
F32 = jnp.float32
NT_DIMS = (((1,), (1,)), ((), ()))


def _dot(a, b):
    return jnp.dot(a, b, preferred_element_type=F32)


def _dot_nt(a, b):
    return lax.dot_general(a, b, NT_DIMS, preferred_element_type=F32)


def _resident(shape):
    zeros = (0,) * len(shape)
    return pl.BlockSpec(tuple(shape), lambda *_: zeros, pipeline_mode=pl.Buffered(1))


def _layer_resident(shape, layer):
    zeros = (0,) * len(shape)
    return pl.BlockSpec((None,) + tuple(shape), lambda *_: (layer,) + zeros,
                        pipeline_mode=pl.Buffered(1))


def _cond_row(tile, mod_row0, tiles_per_seq):
    if mod_row0 == 0:
        return 0
    return mod_row0 + tile // tiles_per_seq


def _params(n_grid_dims):
    return pltpu.CompilerParams(
        dimension_semantics=("arbitrary",) * n_grid_dims,
        vmem_limit_bytes=VMEM_LIMIT_BYTES,
    )


def _cast_rows(src_refs, dst_refs):
    for src, dst in zip(src_refs, dst_refs, strict=True):
        dst[...] = src[...].astype(BF16)


def _row_chunk_specs(stacked, layer, n_steps):
    in_specs, out_specs, out_shapes = [], [], []
    for w in stacked:
        _, n_rows, n_cols = w.shape
        chunk = n_rows // n_steps
        assert chunk * n_steps == n_rows and chunk % 16 == 0
        in_specs.append(pl.BlockSpec((None, chunk, n_cols), lambda i: (layer, i, 0)))
        out_specs.append(pl.BlockSpec((chunk, n_cols), lambda i: (i, 0)))
        out_shapes.append(jax.ShapeDtypeStruct((n_rows, n_cols), BF16))
    return in_specs, out_specs, out_shapes


MOD_COL_TILE = 1536


def _mod_kernel(cond_ref, w_ref, b_ref, o_ref):
    c = cond_ref[...]
    s = (c * jax.nn.sigmoid(c)).astype(BF16)
    o_ref[0] = _dot(s, w_ref[0].astype(BF16)) + b_ref[pl.ds(pl.program_id(0), 1), :]


def _modulation(cond, w_mod, b_mod):
    n_cols = 6 * D_MODEL
    return pl.pallas_call(
        _mod_kernel,
        grid=(DEPTH, n_cols // MOD_COL_TILE),
        in_specs=[
            pl.BlockSpec((COND_ROWS, D_MODEL), lambda l, j: (0, 0)),
            pl.BlockSpec((1, D_MODEL, MOD_COL_TILE), lambda l, j: (l, 0, j)),
            pl.BlockSpec((DEPTH, MOD_COL_TILE), lambda l, j: (0, j)),
        ],
        out_specs=pl.BlockSpec((1, COND_ROWS, MOD_COL_TILE), lambda l, j: (l, 0, j)),
        out_shape=jax.ShapeDtypeStruct((DEPTH, COND_ROWS, n_cols), F32),
        compiler_params=_params(2),
        name="modulation",
    )(cond, w_mod, b_mod)


UNION_ROWS = WIN_R + ROWS_PER_CHUNK
N_CHUNK_KINDS = 3
assert ROWS_PER_CHUNK == WIN_R // 2


def _write_score_tables(rpb_ref, base, o_ref):
    shape = (GRID_W, 2 * GRID_W)
    q = lax.broadcasted_iota(jnp.int32, shape, 0)
    lane = lax.broadcasted_iota(jnp.int32, shape, 1)
    kc = lane & (GRID_W - 1)
    col_start = jnp.clip(q - WIN_C // 2, 0, GRID_W - WIN_C)
    valid = (kc >= col_start) & (kc < col_start + WIN_C)
    neg_inf = jnp.full(shape, -jnp.inf, F32)
    sub = lax.broadcasted_iota(jnp.int32, (8, 2 * GRID_W), 0)
    lane8 = lax.broadcasted_iota(jnp.int32, (8, 2 * GRID_W), 1)
    pad = GRID_W // 2
    t = lane8 - pad - sub + (WIN_C - 1)
    per_row = []
    for j in range(RPB_ROWS):
        tile = jnp.full((8, 2 * GRID_W), -jnp.inf, F32)
        for m in range(RPB_COLS):
            tile = jnp.where(t == m, rpb_ref[base + j * RPB_COLS + m], tile)
        groups = []
        for a in range(GRID_W // 8):
            low = pltpu.roll(tile, (8 * a - pad) % (2 * GRID_W), 1)
            high = pltpu.roll(tile, 8 * a - pad + GRID_W, 1)
            groups.append(jnp.where(lane8 < GRID_W, low, high))
        per_row.append(jnp.where(valid, jnp.concatenate(groups, axis=0), neg_inf))
    chunk_kinds = (
        [(0, g) for g in range(ROWS_PER_CHUNK)],
        [(g, WIN_R // 2) for g in range(ROWS_PER_CHUNK)],
        [(ROWS_PER_CHUNK, WIN_R // 2 + g) for g in range(ROWS_PER_CHUNK)],
    )
    for kind, rows in enumerate(chunk_kinds):
        for g, (offset, d) in enumerate(rows):
            for pair in range(UNION_ROWS // 2):
                halves = []
                for j in (2 * pair, 2 * pair + 1):
                    i = j - offset
                    halves.append(per_row[i + WIN_R - 1 - d] if 0 <= i < WIN_R else neg_inf)
                o_ref[0, kind, g * GRID_W:(g + 1) * GRID_W, pair * 128:(pair + 1) * 128] = (
                    jnp.where(lane < GRID_W, halves[0], halves[1]))


TABLE_BLOCK = (1, N_CHUNK_KINDS, ROWS_PER_CHUNK * GRID_W, UNION_ROWS * GRID_W)


def _transpose_cast_kernel(w_ref, o_ref):
    o_ref[...] = w_ref[...].T.astype(BF16)


def _transposed_kv_weights(w_in):
    return pl.pallas_call(
        _transpose_cast_kernel,
        grid=(DEPTH, 2),
        in_specs=[pl.BlockSpec((None, D_MODEL, ATTN_WIDTH), lambda l, j: (l, 0, 1 + j))],
        out_specs=pl.BlockSpec((None, ATTN_WIDTH, D_MODEL), lambda l, j: (l, j, 0)),
        out_shape=jax.ShapeDtypeStruct((DEPTH, 2 * ATTN_WIDTH, D_MODEL), BF16),
        compiler_params=_params(2),
        name="kv_weight_transpose",
    )(w_in)


def _inproj_kernel(x_ref, mod_ref, w_ref, wkv_t_ref, *rest, mod_row0, tiles_per_seq, seqs_per_tile,
                   n_prev, n_cast, fuse_attn, table_layer, layer):
    cast_src = rest[n_prev:n_prev + n_cast]
    n_in = n_prev + n_cast + (table_layer is not None) + 2 * fuse_attn
    outs = rest[n_in:]
    if fuse_attn:
        first_ref, kt_ref, vt_ref, z_ref, ga_ref, gc_ref = outs[:6]
        n_main = 6
    else:
        first_ref, kt_ref, vt_ref, z_ref, bg_ref, ga_ref, gc_ref = outs[:7]
        n_main = 7
    _cast_rows(cast_src, outs[n_main:n_main + n_cast])
    i = pl.program_id(0)
    row = _cond_row(i, mod_row0, tiles_per_seq)
    sh1 = mod_ref[0, pl.ds(row, 1), 0:D_MODEL]
    sc1 = mod_ref[0, pl.ds(row, 1), D_MODEL:2 * D_MODEL]
    h = (x_ref[...] * (1.0 + sc1) + sh1).astype(BF16)

    def proj(lo, hi):
        return _dot(h, w_ref[:, lo:hi])

    a = ATTN_WIDTH
    c0 = 3 * ATTN_WIDTH
    g0 = c0 + 3 * CONV_WIDTH
    rows = INPROJ_TILE // seqs_per_tile
    head = lambda hd: slice(hd * HEAD_DIM, (hd + 1) * HEAD_DIM)
    seq = lambda s: slice(s * rows, (s + 1) * rows)

    q = proj(0, a) * SCALE
    kv_t = []
    for lo, ref in ((0, kt_ref), (a, vt_ref)):
        y_t = _dot_nt(wkv_t_ref[lo:lo + a, :], h)
        kv_t.append(y_t)
        for s in range(seqs_per_tile):
            for hd in range(N_HEADS):
                ref[s, 0, hd] = y_t[head(hd), seq(s)].astype(ref.dtype)

    def gate_a():
        ga_ref[...] = proj(g0, g0 + D_MODEL).astype(BF16)

    def gate_c():
        gc_ref[...] = proj(g0 + D_MODEL, g0 + 2 * D_MODEL).astype(BF16)

    def conv_input():
        u = proj(c0, c0 + CONV_WIDTH)
        cg = proj(c0 + 2 * CONV_WIDTH, c0 + 3 * CONV_WIDTH)
        z_ref[...] = (cg * u).astype(BF16)

    def conv_gate():
        bg_ref[...] = proj(c0 + CONV_WIDTH, c0 + 2 * CONV_WIDTH).astype(BF16)

    def score_tables():
        base = (table_layer * N_HEADS + i) * (RPB_ROWS * RPB_COLS)
        _write_score_tables(rest[n_in - 1], base, outs[-1])

    conv_state = {}

    def conv_product():
        conv_state["z"] = (proj(c0 + 2 * CONV_WIDTH, c0 + 3 * CONV_WIDTH)
                           * proj(c0, c0 + CONV_WIDTH))

    def gated_conv():
        cw_ref, cb_ref = rest[n_in - 2], rest[n_in - 1].at[layer:layer + 1]
        z = conv_state.pop("z")
        seq_pos = lax.broadcasted_iota(jnp.int32, z.shape, 0) & (rows - 1)
        z_before = jnp.where(seq_pos == 0, 0.0, pltpu.roll(z, 1, 0))
        z_after = jnp.where(seq_pos == rows - 1, 0.0, pltpu.roll(z, INPROJ_TILE - 1, 0))
        conv = (z_before * cw_ref[0:1, :] + z * cw_ref[1:2, :] + z_after * cw_ref[2:3, :]
                + cb_ref[...])
        z_ref[...] = (proj(c0 + CONV_WIDTH, c0 + 2 * CONV_WIDTH) * conv).astype(BF16)

    if fuse_attn:
        def piece(ref, col0, part):
            lo = part * GATE_PIECE
            def emit():
                ref[:, lo:lo + GATE_PIECE] = proj(col0 + lo, col0 + lo + GATE_PIECE).astype(BF16)
            return emit

        parts = range(D_MODEL // GATE_PIECE)
        remaining = ([piece(ga_ref, g0, p) for p in parts]
                     + [piece(gc_ref, g0 + D_MODEL, p) for p in parts]
                     + [conv_product, gated_conv])
    else:
        remaining = [gate_a, gate_c, conv_input, conv_gate]
    if table_layer is not None:
        remaining.insert(1, score_tables)
    if not fuse_attn:
        for s in range(seqs_per_tile):
            for hd in range(N_HEADS):
                first_ref[s, hd] = q[seq(s), head(hd)].astype(BF16)
        for emit in remaining:
            emit()
        return

    items = [(s, hd) for s in range(seqs_per_tile) for hd in range(N_HEADS)]
    n_pieces = len(remaining)
    done = {}

    def score(n):
        s, hd = items[n]
        sc = _dot(q[seq(s), head(hd)].astype(BF16), kv_t[0][head(hd), seq(s)].astype(BF16))
        while n >= 1 and n_pieces - len(remaining) < (n * n_pieces) // (len(items) - 1):
            remaining.pop(0)()
        return sc

    def normalise(n, sc):
        p = jnp.exp(sc - jnp.max(sc, axis=-1, keepdims=True))
        return p.astype(BF16), jnp.sum(p, axis=-1, keepdims=True)

    def finish(n, prob):
        s, hd = items[n]
        p, denom = prob
        done[hd] = _dot_nt(p, kv_t[1][head(hd), seq(s)].astype(BF16)) / denom
        if hd % 2 == 1:
            first_ref[seq(s), (hd - 1) * HEAD_DIM:(hd + 1) * HEAD_DIM] = jnp.concatenate(
                [done.pop(hd - 1), done.pop(hd)], axis=-1).astype(BF16)

    _software_pipeline(len(items), score, normalise, finish)
    assert not remaining


def _inproj(x, mod, w_in_l, w_kv_t, layer, *, n_seq, seq_len, mod_row0, kv_prev=None, cast_src=(),
            rpb_flat=None, conv=None):
    n_tok = n_seq * seq_len
    n_tiles = n_tok // INPROJ_TILE
    is_ctx = mod_row0 == 0
    if seq_len >= INPROJ_TILE:
        tps, spt, rows = seq_len // INPROJ_TILE, 1, INPROJ_TILE
        head_map = lambda i: (i // tps, 0, i % tps, 0)
        kv_map = lambda i: (i // tps, layer if is_ctx else 0, 0, 0, i % tps)
    else:
        tps, spt, rows = 1, INPROJ_TILE // seq_len, seq_len
        head_map = lambda i: (i, 0, 0, 0)
        kv_map = lambda i: (i, layer if is_ctx else 0, 0, 0, 0)
    head_blk = (spt, N_HEADS, rows, HEAD_DIM)
    fuse_attn = is_ctx and seq_len <= INPROJ_TILE
    kv_blk = (spt, 1, N_HEADS, HEAD_DIM, rows)
    if is_ctx:
        kv_shape = jax.ShapeDtypeStruct((n_seq, DEPTH, N_HEADS, HEAD_DIM, seq_len), F32)
    else:
        kv_shape = jax.ShapeDtypeStruct((n_seq, 1, N_HEADS, HEAD_DIM, seq_len), BF16)
    tok = lambda width: pl.BlockSpec((INPROJ_TILE, width), lambda i: (i, 0))
    tok_shape = lambda width: jax.ShapeDtypeStruct((n_tok, width), BF16)
    in_specs = [
        tok(D_MODEL),
        pl.BlockSpec((1, COND_ROWS, 6 * D_MODEL), lambda i: (layer, 0, 0)),
        _resident((D_MODEL, IN_COLS)),
        _layer_resident((2 * ATTN_WIDTH, D_MODEL), layer),
    ]
    args = [x, mod, w_in_l, w_kv_t]
    aliases = {}
    n_prev = 0
    if kv_prev is not None:
        in_specs += [pl.BlockSpec(memory_space=pl.ANY)] * 2
        args += list(kv_prev)
        aliases = {4: 1, 5: 2}
        n_prev = 2
    cast_in, cast_out, cast_shapes = _row_chunk_specs(cast_src, layer, n_tiles)
    args += list(cast_src)
    table_in, table_out, table_shape = [], [], []
    if rpb_flat is not None:
        assert n_tiles == N_HEADS
        table_in = [pl.BlockSpec(memory_space=pltpu.SMEM)]
        table_out = [pl.BlockSpec(TABLE_BLOCK, lambda i: (i, 0, 0, 0))]
        table_shape = [jax.ShapeDtypeStruct((N_HEADS,) + TABLE_BLOCK[1:], F32)]
        args.append(rpb_flat)
    conv_in = []
    if fuse_attn:
        assert rows & (rows - 1) == 0 and conv is not None
        conv_w, conv_b = conv
        conv_in = [_layer_resident(conv_w.shape[1:], layer), _resident(conv_b.shape)]
        args += [conv_w, conv_b]
        main_specs = [tok(ATTN_WIDTH), pl.BlockSpec(kv_blk, kv_map), pl.BlockSpec(kv_blk, kv_map),
                      tok(CONV_WIDTH), tok(D_MODEL), tok(D_MODEL)]
        main_shapes = [tok_shape(ATTN_WIDTH), kv_shape, kv_shape,
                       tok_shape(CONV_WIDTH), tok_shape(D_MODEL), tok_shape(D_MODEL)]
    else:
        main_specs = [pl.BlockSpec(head_blk, head_map), pl.BlockSpec(kv_blk, kv_map),
                      pl.BlockSpec(kv_blk, kv_map),
                      tok(CONV_WIDTH), tok(CONV_WIDTH), tok(D_MODEL), tok(D_MODEL)]
        main_shapes = [jax.ShapeDtypeStruct((n_seq, N_HEADS, seq_len, HEAD_DIM), BF16),
                       kv_shape, kv_shape, tok_shape(CONV_WIDTH), tok_shape(CONV_WIDTH),
                       tok_shape(D_MODEL), tok_shape(D_MODEL)]
    return pl.pallas_call(
        functools.partial(_inproj_kernel, mod_row0=mod_row0, tiles_per_seq=tps, seqs_per_tile=spt,
                          n_prev=n_prev, n_cast=len(cast_src), fuse_attn=fuse_attn,
                          table_layer=None if rpb_flat is None else layer, layer=layer),
        grid=(n_tiles,),
        in_specs=in_specs + cast_in + table_in + conv_in,
        out_specs=main_specs + cast_out + table_out,
        out_shape=main_shapes + cast_shapes + table_shape,
        input_output_aliases=aliases,
        compiler_params=_params(1),
        name="inproj_ctx" if is_ctx else "inproj_lat",
    )(*args)


def _software_pipeline(n_items, score, normalise, finish):
    scores = {0: score(0)}
    probs = {}
    for i in range(n_items + 1):
        if i + 1 < n_items:
            scores[i + 1] = score(i + 1)
        if i < n_items:
            probs[i] = normalise(i, scores.pop(i))
        if i >= 1:
            finish(i - 1, probs.pop(i - 1))


HEADS_PER_STEP = 2


def _lat_attn_kernel(q_ref, kt_ref, vt_ref, kct_ref, vct_ref, tab_ref, o_ref, *, n_rows):
    n_chunks = n_rows // ROWS_PER_CHUNK
    chunk_q = ROWS_PER_CHUNK * GRID_W
    union = UNION_ROWS * GRID_W
    items = [(c, hh) for c in range(n_chunks) for hh in range(HEADS_PER_STEP)]
    done = {}

    def key_cols(c):
        first_key_row = min(max(c * ROWS_PER_CHUNK - WIN_R // 2, 0), n_rows - UNION_ROWS)
        return slice(first_key_row * GRID_W, first_key_row * GRID_W + union)

    def score(i):
        c, hh = items[i]
        kind = 0 if c == 0 else (2 if c == n_chunks - 1 else 1)
        qc = q_ref[0, hh, c * chunk_q:(c + 1) * chunk_q, :]
        s_band = _dot(qc, kt_ref[0, 0, hh, :, key_cols(c)]) + tab_ref[hh, kind]
        s_ctx = _dot(qc, kct_ref[0, 0, hh].astype(BF16))
        return s_band, s_ctx

    def normalise(i, scores):
        s_band, s_ctx = scores
        m = jnp.maximum(jnp.max(s_band, axis=-1, keepdims=True),
                        jnp.max(s_ctx, axis=-1, keepdims=True))
        p_band = jnp.exp(s_band - m)
        p_ctx = jnp.exp(s_ctx - m)
        denom = jnp.sum(p_band, axis=-1, keepdims=True) + jnp.sum(p_ctx, axis=-1, keepdims=True)
        return p_band.astype(BF16), p_ctx.astype(BF16), denom

    def finish(i, prob):
        c, hh = items[i]
        p_band, p_ctx, denom = prob
        o = (_dot_nt(p_band, vt_ref[0, 0, hh, :, key_cols(c)])
             + _dot_nt(p_ctx, vct_ref[0, 0, hh].astype(BF16)))
        done[hh] = o / denom
        if hh == HEADS_PER_STEP - 1:
            o_ref[0, c * chunk_q:(c + 1) * chunk_q, :] = jnp.concatenate(
                [done.pop(h) for h in range(HEADS_PER_STEP)], axis=-1).astype(BF16)

    _software_pipeline(len(items), score, normalise, finish)


def _lat_attention(q, k_t, v_t, cache_k_t, cache_v_t, tables, layer):
    n_seq, _, seq_len, _ = q.shape
    past_len = cache_k_t.shape[4]
    n_rows = seq_len // GRID_W
    assert n_rows % ROWS_PER_CHUNK == 0 and n_rows >= 2 * UNION_ROWS
    hps = HEADS_PER_STEP
    return pl.pallas_call(
        functools.partial(_lat_attn_kernel, n_rows=n_rows),
        grid=(N_HEADS // hps, n_seq),
        in_specs=[
            pl.BlockSpec((1, hps, seq_len, HEAD_DIM), lambda p, b: (b, p, 0, 0)),
            pl.BlockSpec((1, 1, hps, HEAD_DIM, seq_len), lambda p, b: (b, 0, p, 0, 0)),
            pl.BlockSpec((1, 1, hps, HEAD_DIM, seq_len), lambda p, b: (b, 0, p, 0, 0)),
            pl.BlockSpec((1, 1, hps, HEAD_DIM, past_len), lambda p, b: (b, layer, p, 0, 0)),
            pl.BlockSpec((1, 1, hps, HEAD_DIM, past_len), lambda p, b: (b, layer, p, 0, 0)),
            pl.BlockSpec((hps,) + tables.shape[1:], lambda p, b: (p, 0, 0, 0)),
        ],
        out_specs=pl.BlockSpec((1, seq_len, hps * HEAD_DIM), lambda p, b: (b, 0, p)),
        out_shape=jax.ShapeDtypeStruct((n_seq, seq_len, ATTN_WIDTH), BF16),
        compiler_params=_params(2),
        name="attn_lat",
    )(q, k_t, v_t, cache_k_t, cache_v_t, tables)


def _layer_norm(y, g, b):
    mu = jnp.mean(y, axis=-1, keepdims=True)
    yc = y - mu
    var = jnp.mean(yc * yc, axis=-1, keepdims=True)
    return yc * lax.rsqrt(var + LN_EPS) * g + b


def _post_kernel(x_ref, attn_ref, z_ref, zprev_ref, znext_ref, bg_ref, ga_ref, gc_ref, mod_ref,
                 wap_ref, wcp_ref, wo_ref, w1_ref, w2_ref, cw_ref, cb_ref,
                 ln1g_ref, ln1b_ref, b1_ref, b2_ref, ln2g_ref, ln2b_ref, *rest,
                 mod_row0, tiles_per_seq, seq_len, n_cast, layer, conv_done):
    o_ref = rest[n_cast]
    _cast_rows(rest[:n_cast], rest[n_cast + 1:])
    cb_ref, ln1g_ref, ln1b_ref, b1_ref, b2_ref, ln2g_ref, ln2b_ref = (
        r.at[layer:layer + 1] for r in (cb_ref, ln1g_ref, ln1b_ref, b1_ref, b2_ref, ln2g_ref,
                                        ln2b_ref))
    i = pl.program_id(0)
    row = _cond_row(i, mod_row0, tiles_per_seq)
    d = D_MODEL
    g1 = mod_ref[0, pl.ds(row, 1), 2 * d:3 * d]
    sh2 = mod_ref[0, pl.ds(row, 1), 3 * d:4 * d]
    sc2 = mod_ref[0, pl.ds(row, 1), 4 * d:5 * d]
    g2 = mod_ref[0, pl.ds(row, 1), 5 * d:6 * d]

    if conv_done:
        conv_in = z_ref[...]
    else:
        z = z_ref[...].astype(F32)
        prev_row = zprev_ref[...].astype(F32)[HALO_ROWS - 1:HALO_ROWS, :]
        next_row = znext_ref[...].astype(F32)[0:1, :]
        pos = lax.broadcasted_iota(jnp.int32, z.shape, 0)
        seq_pos = (i * TOKEN_TILE + pos) & (seq_len - 1)
        z_before = jnp.where(pos == 0, prev_row, pltpu.roll(z, 1, 0))
        z_before = jnp.where(seq_pos == 0, 0.0, z_before)
        z_after = jnp.where(pos == TOKEN_TILE - 1, next_row, pltpu.roll(z, TOKEN_TILE - 1, 0))
        z_after = jnp.where(seq_pos == seq_len - 1, 0.0, z_after)
        conv = (z_before * cw_ref[0:1, :] + z * cw_ref[1:2, :] + z_after * cw_ref[2:3, :]
                + cb_ref[...])
        conv_in = (bg_ref[...].astype(F32) * conv).astype(BF16)

    def merge(rs):
        attn_p = _dot(attn_ref[rs, :], wap_ref[...])
        conv_p = _dot(conv_in[rs, :], wcp_ref[...])
        return (jax.nn.sigmoid(ga_ref[rs, :].astype(F32)) * attn_p
                + jax.nn.sigmoid(gc_ref[rs, :].astype(F32)) * conv_p).astype(BF16)

    def mix_norm(merged, rs):
        mix = _dot(merged, wo_ref[...])
        x1 = _layer_norm(ALPHA * x_ref[rs, :] + g1 * mix, ln1g_ref[...], ln1b_ref[...])
        return x1, (x1 * (1.0 + sc2) + sh2).astype(BF16)

    def mlp_chunk(h2, c):
        cols = slice(c * FF_CHUNK, (c + 1) * FF_CHUNK)
        hidden = jnp.maximum(_dot(h2, w1_ref[:, cols]) + b1_ref[:, cols], 0.0)
        return _dot((hidden * hidden).astype(BF16), w2_ref[cols, :])

    def finish(x1, f, rs):
        o_ref[rs, :] = _layer_norm(ALPHA * x1 + g2 * f, ln2g_ref[...], ln2b_ref[...])

    groups = [slice(r0, r0 + POST_ROW_GROUP) for r0 in range(0, TOKEN_TILE, POST_ROW_GROUP)]
    n_chunks = D_FF // FF_CHUNK
    x1, h2 = mix_norm(merge(groups[0]), groups[0])
    f = b2_ref[...] + mlp_chunk(h2, 0)
    for n, rs in enumerate(groups):
        nxt = groups[n + 1] if n + 1 < len(groups) else None
        if nxt is not None:
            merged_next = merge(nxt)
        f = f + mlp_chunk(h2, 1)
        if nxt is not None:
            x1_next, h2_next = mix_norm(merged_next, nxt)
        for c in range(2, n_chunks):
            f = f + mlp_chunk(h2, c)
        if nxt is not None:
            f_next = b2_ref[...] + mlp_chunk(h2_next, 0)
        finish(x1, f, rs)
        if nxt is not None:
            x1, h2, f = x1_next, h2_next, f_next


def _post(x, attn, z, bg, ga, gc, mod, mats, weights, layer, *, seq_len, mod_row0, cast_src=(),
          cast_layer=None):
    n_tok = x.shape[0]
    conv_done = bg is None
    if conv_done:
        bg = z
    assert seq_len & (seq_len - 1) == 0
    tps = max(seq_len // TOKEN_TILE, 1)
    n_tiles = n_tok // TOKEN_TILE
    halo_per_tile = TOKEN_TILE // HALO_ROWS
    n_halo_blocks = n_tok // HALO_ROWS
    tok = lambda width: pl.BlockSpec((TOKEN_TILE, width), lambda i: (i, 0))
    in_specs = [
        tok(D_MODEL), tok(ATTN_WIDTH), tok(CONV_WIDTH),
        pl.BlockSpec((HALO_ROWS, CONV_WIDTH), lambda i: (jnp.maximum(i * halo_per_tile - 1, 0), 0)),
        pl.BlockSpec((HALO_ROWS, CONV_WIDTH),
                     lambda i: (jnp.minimum((i + 1) * halo_per_tile, n_halo_blocks - 1), 0)),
        tok(CONV_WIDTH), tok(D_MODEL), tok(D_MODEL),
        pl.BlockSpec((1, COND_ROWS, 6 * D_MODEL), lambda i: (layer, 0, 0)),
    ]
    per_layer = [weights["conv_w"]] + weights["vectors"]
    in_specs += [_resident(m.shape) for m in mats]
    in_specs += [_layer_resident(weights["conv_w"].shape[1:], layer)]
    in_specs += [_resident(v.shape) for v in weights["vectors"]]
    cast_in, cast_out, cast_shapes = _row_chunk_specs(cast_src, cast_layer, n_tiles)
    return pl.pallas_call(
        functools.partial(_post_kernel, mod_row0=mod_row0, tiles_per_seq=tps, seq_len=seq_len,
                          n_cast=len(cast_src), layer=layer, conv_done=conv_done),
        grid=(n_tiles,),
        in_specs=in_specs + cast_in,
        out_specs=[tok(D_MODEL)] + cast_out,
        out_shape=[jax.ShapeDtypeStruct((n_tok, D_MODEL), F32)] + cast_shapes,
        compiler_params=_params(1),
        name="post_ctx" if mod_row0 == 0 else "post_lat",
    )(x, attn, z, z, z, bg, ga, gc, mod, *mats, *per_layer, *cast_src)


def kernel(x_prompt, x_sample, cache_k, cache_v, c, c_ctx, w_mod, b_mod, w_in, rpb, conv_w, conv_b,
           w_attn_proj, w_conv_proj, w_o, ln1_g, ln1_b, w1, b1, w2, b2, ln2_g, ln2_b):
    batch, seq, _ = x_prompt.shape
    dec_batch, dec_seq, _ = x_sample.shape

    cond = jnp.concatenate(
        [c_ctx[None, :], c, jnp.zeros((COND_ROWS - 1 - dec_batch, D_MODEL), F32)], axis=0)
    mod = _modulation(cond, w_mod, b_mod)
    rpb_flat = rpb.reshape(-1)

    w_in_l = w_in[0].astype(BF16)
    post_mats_f32 = [w_attn_proj, w_conv_proj, w_o, w1, w2]
    w_kv_t = _transposed_kv_weights(w_in)
    cache_k_t = jnp.swapaxes(cache_k, -1, -2)
    cache_v_t = jnp.swapaxes(cache_v, -1, -2)
    weights = {
        "conv_w": conv_w,
        "vectors": [conv_b, ln1_g, ln1_b, b1, b2, ln2_g, ln2_b],
    }

    xp = x_prompt.reshape(batch * seq, D_MODEL)
    xs = x_sample.reshape(dec_batch * dec_seq, D_MODEL)
    kv = None
    for layer in range(DEPTH):
        attn, new_k_t, new_v_t, gated_conv, ga, gc, *mats = _inproj(
            xp, mod, w_in_l, w_kv_t, layer, n_seq=batch, seq_len=seq, mod_row0=0, kv_prev=kv,
            cast_src=post_mats_f32, conv=(conv_w, conv_b))
        kv = (new_k_t, new_v_t)
        last = layer == DEPTH - 1
        xp, *w_in_next = _post(xp, attn, gated_conv, None, ga, gc, mod, mats, weights, layer,
                               seq_len=seq, mod_row0=0, cast_src=() if last else (w_in,),
                               cast_layer=layer + 1)

        q, k_t, v_t, z, bg, ga, gc, tables = _inproj(
            xs, mod, w_in_l, w_kv_t, layer, n_seq=dec_batch, seq_len=dec_seq, mod_row0=1,
            rpb_flat=rpb_flat)
        attn = _lat_attention(q, k_t, v_t, cache_k_t, cache_v_t, tables, layer)
        (xs,) = _post(xs, attn.reshape(dec_batch * dec_seq, ATTN_WIDTH), z, bg, ga, gc, mod, mats,
                      weights, layer, seq_len=dec_seq, mod_row0=1)
        if not last:
            (w_in_l,) = w_in_next

    new_k = jnp.swapaxes(kv[0], -1, -2)
    new_v = jnp.swapaxes(kv[1], -1, -2)
    return (xp.reshape(batch, seq, D_MODEL), xs.reshape(dec_batch, dec_seq, D_MODEL), new_k, new_v)
```
